```python
import math
import jax, jax.numpy as jnp
from jax import lax
import numpy as np

D_MODEL = 2048
BATCH = 32
SEQ = 256
DEPTH = 2
DEC_BATCH = 2
DEC_SEQ = 1024
PAST_LEN = 256

GRID_W = 64
D_MIX = D_MODEL
D_A = D_MIX // 2
D_B = D_MIX // 4
D_C = D_MIX - D_A - D_B
LRU_BLOCKS = 8
LRU_BW = D_A // LRU_BLOCKS
LRU_C = 8.0
CONV_A_W = 4
FNET_GROUPS = 4
FNET_GW = D_B // FNET_GROUPS
CONV_C_W = 31
D_IN = 2 * D_A + D_B + 2 * D_C
D_FF = ((8 * D_MODEL + 3 * 256 - 1) // (3 * 256)) * 256
N_MOD = 6
EPS = 1e-6

kernel_name = "hybrid_lru_fnet_conformer_diffusion_step"


def _rmsnorm(x, g):
    xf = x.astype(jnp.float32)
    y = xf * lax.rsqrt(jnp.mean(xf * xf, axis=-1, keepdims=True) + EPS)
    return (y * g.astype(jnp.float32)).astype(x.dtype)


def _dwconv(x, w, b, pad_left, pad_right):
    C = x.shape[-1]
    y = lax.conv_general_dilated(x, w[:, None, :].astype(x.dtype), window_strides=(1,),
                                 padding=[(pad_left, pad_right)],
                                 dimension_numbers=('NWC', 'WIO', 'NWC'),
                                 feature_group_count=C)
    return y + b.astype(x.dtype)


def _lin_scan(a, u, h0, reverse):
    def op(l, r):
        return (l[0] * r[0], r[0] * l[1] + r[1])
    a_c, u_c = lax.associative_scan(op, (a, u), reverse=reverse, axis=1)
    return a_c * h0[:, None, :] + u_c


def _rg_lru(xa, h0, wa, ba, wx, bx, lam):
    B_, S_, _ = xa.shape
    xblk = xa.reshape(B_, S_, LRU_BLOCKS, LRU_BW)
    f32 = jnp.float32
    gate_a = jnp.einsum('bsnk,dnkj->dbsnj', xblk, wa.astype(f32)).reshape(2, B_, S_, D_A) + ba.astype(f32)[:, None, None, :]
    gate_x = jnp.einsum('bsnk,dnkj->dbsnj', xblk, wx.astype(f32)).reshape(2, B_, S_, D_A) + bx.astype(f32)[:, None, None, :]
    log_a = -LRU_C * jax.nn.sigmoid(gate_a) * jax.nn.softplus(-lam.astype(f32))[:, None, None, :]
    a = jnp.exp(log_a)
    u = jnp.sqrt(-jnp.expm1(2.0 * log_a)) * jax.nn.sigmoid(gate_x) * xa[None]
    h_f = _lin_scan(a[0], u[0], h0[:, 0], False)
    h_b = _lin_scan(a[1], u[1], h0[:, 1], True)
    h_last = jnp.stack([h_f[:, -1], h_b[:, 0]], axis=1)
    return h_f + h_b, h_last


def _fourier(xb):
    B_, S_, _ = xb.shape
    z = xb.astype(jnp.float32).reshape(B_, S_, FNET_GROUPS, FNET_GW)
    z = jnp.fft.fft2(z, axes=(1, 3), norm='ortho').real
    return z.reshape(B_, S_, D_B)


def _conformer_conv(xc, w, b, g, beta):
    v = xc[..., :D_C] * jax.nn.sigmoid(xc[..., D_C:])
    half = (CONV_C_W - 1) // 2
    v = _dwconv(v, w, b, half, half).astype(jnp.float32)
    mu = jnp.mean(v, axis=-1, keepdims=True)
    var = jnp.mean(jnp.square(v - mu), axis=-1, keepdims=True)
    v = (v - mu) * lax.rsqrt(var + EPS) * g.astype(jnp.float32) + beta.astype(jnp.float32)
    return jax.nn.silu(v).astype(xc.dtype)


def _grid_pos_embed(L, rows, dtype):
    t = jnp.arange(rows * GRID_W)
    r = (t // GRID_W).astype(jnp.float32)
    col = (t % GRID_W).astype(jnp.float32)
    nf = D_MODEL // 4
    omega = 1.0 / (10000.0 ** (jnp.arange(nf, dtype=jnp.float32) / nf))
    def enc(p):
        ang = p[:, None] * omega[None, :]
        return jnp.concatenate([jnp.sin(ang), jnp.cos(ang)], axis=-1)
    return jnp.concatenate([enc(r), enc(col)], axis=-1).astype(dtype)


def _layer(x, mod, h0, norm_mix, norm_ffn, w_in, conv_a_w, conv_a_b, lru_wa, lru_ba, lru_wx,
           lru_bx, lru_lam, conv_c_w, conv_c_b, ln_c_g, ln_c_b, out_norm, w_out, w_gu, w_down):
    shift1, scale1, gate1, shift2, scale2, gate2 = [mod[:, i][:, None, :] for i in range(N_MOD)]
    h = _rmsnorm(x, norm_mix) * (1 + scale1) + shift1
    u = h @ w_in
    xa, ya, xb, xc = jnp.split(u, [D_A, 2 * D_A, 2 * D_A + D_B], axis=-1)
    xa = _dwconv(xa, conv_a_w, conv_a_b, 2, 1)
    ha, h_last = _rg_lru(xa.astype(jnp.float32), h0, lru_wa, lru_ba, lru_wx, lru_bx, lru_lam)
    out_a = (ha * jax.nn.gelu(ya.astype(jnp.float32))).astype(x.dtype)
    out_b = _fourier(xb).astype(x.dtype)
    out_c = _conformer_conv(xc, conv_c_w, conv_c_b, ln_c_g, ln_c_b)
    g_a, g_b, g_c = jnp.split(out_norm, [D_A, D_A + D_B])
    m = jnp.concatenate([_rmsnorm(out_a, g_a), _rmsnorm(out_b, g_b), _rmsnorm(out_c, g_c)], axis=-1)
    x = x + gate1 * (m @ w_out)
    h = _rmsnorm(x, norm_ffn) * (1 + scale2) + shift2
    g_, up = jnp.split(h @ w_gu, 2, axis=-1)
    x = x + gate2 * ((jax.nn.silu(g_) * up) @ w_down)
    return x, h_last


def setup_inputs(seed: int = 0) -> dict:
    key = jax.random.key(seed)
    ks = jax.random.split(key, 32)
    f32 = jnp.float32
    def nrm(k, shape, scale):
        return jax.random.normal(k, shape, f32) * scale
    a0 = jax.random.uniform(ks[12], (DEPTH, 2, D_A), f32, minval=0.9, maxval=0.999)
    return {
        "x_prompt": nrm(ks[0], (BATCH, SEQ, D_MODEL), 1.0),
        "x_sample": nrm(ks[1], (DEC_BATCH, DEC_SEQ, D_MODEL), 1.0),
        "c": nrm(ks[2], (DEC_BATCH, D_MODEL), 1.0),
        "state_lru": nrm(ks[3], (DEC_BATCH, DEPTH, 2, D_A), 1.0),
        "c_ctx": nrm(ks[4], (D_MODEL,), 1.0),
        "w_mod": nrm(ks[5], (DEPTH, D_MODEL, N_MOD * D_MODEL), 0.5 * D_MODEL ** -0.5),
        "b_mod": nrm(ks[6], (DEPTH, N_MOD * D_MODEL), 0.02),
        "norm_mix": 1.0 + nrm(ks[7], (DEPTH, D_MODEL), 0.02),
        "norm_ffn": 1.0 + nrm(ks[8], (DEPTH, D_MODEL), 0.02),
        "w_in": nrm(ks[9], (DEPTH, D_MODEL, D_IN), D_MODEL ** -0.5),
        "conv_a_w": nrm(ks[10], (DEPTH, CONV_A_W, D_A), CONV_A_W ** -0.5),
        "conv_a_b": nrm(ks[11], (DEPTH, D_A), 0.02),
        "lru_wa": nrm(ks[13], (DEPTH, 2, LRU_BLOCKS, LRU_BW, LRU_BW), LRU_BW ** -0.5),
        "lru_ba": nrm(ks[14], (DEPTH, 2, D_A), 0.02),
        "lru_wx": nrm(ks[15], (DEPTH, 2, LRU_BLOCKS, LRU_BW, LRU_BW), LRU_BW ** -0.5),
        "lru_bx": nrm(ks[16], (DEPTH, 2, D_A), 0.02),
        "lru_lam": jnp.log(a0) - jnp.log1p(-a0),
        "conv_c_w": nrm(ks[17], (DEPTH, CONV_C_W, D_C), CONV_C_W ** -0.5),
        "conv_c_b": nrm(ks[18], (DEPTH, D_C), 0.02),
        "ln_c_g": 1.0 + nrm(ks[19], (DEPTH, D_C), 0.02),
        "ln_c_b": nrm(ks[20], (DEPTH, D_C), 0.02),
        "out_norm": 1.0 + nrm(ks[21], (DEPTH, D_MIX), 0.02),
        "w_out": nrm(ks[22], (DEPTH, D_MIX, D_MODEL), D_MIX ** -0.5),
        "w_gu": nrm(ks[23], (DEPTH, D_MODEL, 2 * D_FF), D_MODEL ** -0.5),
        "w_down": nrm(ks[24], (DEPTH, D_FF, D_MODEL), D_FF ** -0.5),
        "final_norm": 1.0 + nrm(ks[25], (D_MODEL,), 0.02),
    }


def reference(x_prompt, x_sample, c, state_lru, c_ctx, w_mod, b_mod, norm_mix, norm_ffn, w_in,
              conv_a_w, conv_a_b, lru_wa, lru_ba, lru_wx, lru_bx, lru_lam, conv_c_w, conv_c_b,
              ln_c_g, ln_c_b, out_norm, w_out, w_gu, w_down, final_norm):
    L = x_sample.shape[1]
    ROWS = L // GRID_W
    xs = x_sample + _grid_pos_embed(L, ROWS, x_sample.dtype)[None]
    xp = x_prompt
    h0_ctx = jnp.zeros((x_prompt.shape[0], 2, D_A), jnp.float32)
    ctx_states = []
    for l in range(DEPTH):
        p = (norm_mix[l], norm_ffn[l], w_in[l], conv_a_w[l], conv_a_b[l], lru_wa[l], lru_ba[l],
             lru_wx[l], lru_bx[l], lru_lam[l], conv_c_w[l], conv_c_b[l], ln_c_g[l], ln_c_b[l],
             out_norm[l], w_out[l], w_gu[l], w_down[l])
        mod_ctx = (jax.nn.silu(c_ctx)[None] @ w_mod[l] + b_mod[l]).reshape(1, N_MOD, D_MODEL)
        mod_lat = (jax.nn.silu(c) @ w_mod[l] + b_mod[l]).reshape(-1, N_MOD, D_MODEL)
        xp, h_last = _layer(xp, mod_ctx, h0_ctx, *p)
        ctx_states.append(h_last.astype(x_prompt.dtype))
        xs, _ = _layer(xs, mod_lat, state_lru[:, l].astype(jnp.float32), *p)
    y_prompt = _rmsnorm(xp, final_norm)
    y_sample = _rmsnorm(xs, final_norm)
    new_state_lru = jnp.stack(ctx_states, axis=1)
    return (y_prompt, y_sample, new_state_lru)
```

```python
import functools

import numpy as np
import jax
import jax.numpy as jnp
from jax import lax
from jax.experimental import pallas as pl
from jax.experimental.pallas import tpu as pltpu

f32 = jnp.float32
bf16 = jnp.bfloat16

D_MODEL = 2048
D_A = 1024
D_B = 512
D_C = 512
LRU_BLOCKS = 8
LRU_BW = 128
LRU_C = 8.0
CONV_A_W = 4
FNET_GROUPS = 4
FNET_GW = 128
CONV_C_W = 31
D_IN = 2 * D_A + D_B + 2 * D_C
D_FF = 5632
N_MOD = 6
GRID_W = 64
EPS = 1e-6

LANES = 128
SUBLANES = 8
MIB = 1024 * 1024

SHIFT1, SCALE1, GATE1, SHIFT2, SCALE2, GATE2 = range(N_MOD)


def _cparams(sem, vmem_mib):
    return pltpu.CompilerParams(dimension_semantics=sem, vmem_limit_bytes=vmem_mib * MIB)


def _sigmoid(x):
    return jax.nn.sigmoid(x)


def _rms_scale(x, g):
    ms = jnp.mean(x * x, axis=-1, keepdims=True)
    return x * lax.rsqrt(ms + EPS) * g


def _normmod(x, g, scale, shift):
    return _rms_scale(x, g) * (1.0 + scale) + shift


def _gelu_tanh(x):
    return x * (0.5 * (1.0 + jnp.tanh(np.sqrt(2.0 / np.pi).astype(np.float32) * (x + 0.044715 * (x * x * x)))))


MOD_TN = 1024


def _mod_body(cv_ref, w_ref, b_ref, o_ref):
    cv = cv_ref[...]
    s = (cv * _sigmoid(cv)).astype(bf16)
    o_ref[...] = jnp.dot(s, w_ref[...].astype(bf16), preferred_element_type=f32) + b_ref[...]


def _mod_call(cv, w_mod, b_mod):
    depth = w_mod.shape[0]
    n = N_MOD * D_MODEL
    return pl.pallas_call(
        _mod_body,
        grid=(depth, n // MOD_TN),
        in_specs=[
            pl.BlockSpec((SUBLANES, D_MODEL), lambda l, j: (0, 0)),
            pl.BlockSpec((None, D_MODEL, MOD_TN), lambda l, j: (l, 0, j)),
            pl.BlockSpec((None, 1, MOD_TN), lambda l, j: (l, 0, j)),
        ],
        out_specs=pl.BlockSpec((None, SUBLANES, MOD_TN), lambda l, j: (l, 0, j)),
        out_shape=jax.ShapeDtypeStruct((depth, SUBLANES, n), f32),
        compiler_params=_cparams(("arbitrary", "arbitrary"), 40),
        name="mod_vectors",
    )(cv, w_mod, b_mod.reshape(depth, 1, n))


PRE_TM = 512


def _mod_spec(tm, seq, nmod):
    if nmod == 1:
        return pl.BlockSpec((None, N_MOD, D_MODEL), lambda i, *_: (0, 0, 0))
    return pl.BlockSpec((None, N_MOD, D_MODEL), lambda i, *_: ((i * tm) // seq, 0, 0))


def _pre_ctx_body(x_ref, g_ref, mod_ref, h_ref):
    h_ref[...] = _normmod(x_ref[...], g_ref[...], mod_ref[SCALE1:SCALE1 + 1, :],
                          mod_ref[SHIFT1:SHIFT1 + 1, :]).astype(bf16)


def _pre_ctx_call(x, g, mod, seq):
    rows = x.shape[0]
    return pl.pallas_call(
        _pre_ctx_body,
        grid=(rows // PRE_TM,),
        in_specs=[
            pl.BlockSpec((PRE_TM, D_MODEL), lambda i: (i, 0)),
            pl.BlockSpec((1, D_MODEL), lambda i: (0, 0)),
            _mod_spec(PRE_TM, seq, mod.shape[0]),
        ],
        out_specs=pl.BlockSpec((PRE_TM, D_MODEL), lambda i: (i, 0)),
        out_shape=jax.ShapeDtypeStruct((rows, D_MODEL), bf16),
        compiler_params=_cparams(("arbitrary",), 32),
        name="pre_ctx",
    )(x, g, mod)


def _pre_lat_body(x_ref, om_ref, g_ref, mod_ref, x0_ref, h_ref, *, tiles_per_seq):
    nf = D_MODEL // 4
    grid_rows = PRE_TM // GRID_W
    i = pl.program_id(0)
    r_base = (i % tiles_per_seq) * grid_rows
    om = om_ref[...]
    ang_r = (r_base + lax.broadcasted_iota(jnp.int32, (grid_rows, nf), 0)).astype(f32) * om
    sin_r, cos_r = jnp.sin(ang_r), jnp.cos(ang_r)
    ang_c = lax.broadcasted_iota(jnp.int32, (GRID_W, nf), 0).astype(f32) * om
    sin_c, cos_c = jnp.sin(ang_c), jnp.cos(ang_c)
    g = g_ref[...]
    scale = mod_ref[SCALE1:SCALE1 + 1, :]
    shift = mod_ref[SHIFT1:SHIFT1 + 1, :]
    for q in range(grid_rows):
        rows = slice(q * GRID_W, (q + 1) * GRID_W)
        pos = jnp.concatenate([jnp.broadcast_to(sin_r[q:q + 1], (GRID_W, nf)),
                               jnp.broadcast_to(cos_r[q:q + 1], (GRID_W, nf)), sin_c, cos_c], axis=-1)
        x0 = x_ref[rows, :] + pos
        x0_ref[rows, :] = x0
        h_ref[rows, :] = _normmod(x0, g, scale, shift).astype(bf16)


def _pre_lat_call(x, omega, g, mod, seq):
    rows = x.shape[0]
    assert seq % PRE_TM == 0 and PRE_TM % GRID_W == 0
    row_spec = pl.BlockSpec((PRE_TM, D_MODEL), lambda i: (i, 0))
    return pl.pallas_call(
        functools.partial(_pre_lat_body, tiles_per_seq=seq // PRE_TM),
        grid=(rows // PRE_TM,),
        in_specs=[
            row_spec,
            pl.BlockSpec((1, D_MODEL // 4), lambda i: (0, 0)),
            pl.BlockSpec((1, D_MODEL), lambda i: (0, 0)),
            _mod_spec(PRE_TM, seq, mod.shape[0]),
        ],
        out_specs=[row_spec, row_spec],
        out_shape=[jax.ShapeDtypeStruct((rows, D_MODEL), f32), jax.ShapeDtypeStruct((rows, D_MODEL), bf16)],
        compiler_params=_cparams(("arbitrary",), 40),
        name="pre_lat",
    )(x, omega, g, mod)


INPROJ_TM = 1024
INPROJ_TN = 512


def _inproj_body(h_ref, w_ref, o_ref):
    o_ref[...] = jnp.dot(h_ref[...], w_ref[...], preferred_element_type=f32)


def _inproj_call(h, w):
    rows = h.shape[0]
    return pl.pallas_call(
        _inproj_body,
        grid=(rows // INPROJ_TM, D_IN // INPROJ_TN),
        in_specs=[
            pl.BlockSpec((INPROJ_TM, D_MODEL), lambda i, j: (i, 0)),
            pl.BlockSpec((D_MODEL, INPROJ_TN), lambda i, j: (0, j)),
        ],
        out_specs=pl.BlockSpec((INPROJ_TM, INPROJ_TN), lambda i, j: (i, j)),
        out_shape=jax.ShapeDtypeStruct((rows, D_IN), f32),
        compiler_params=_cparams(("arbitrary", "arbitrary"), 40),
        name="in_proj",
    )(h, w)


A_RC = 64
A_OC = 32
A_HALO = 8


def _mixa_body(xa_ref, ya_ref, h0_ref, cw_ref, cb_ref, wg_ref, bg_ref, lam_ref, gn_ref,
               ma_ref, hl_ref, pad_ref, af_ref, uf_ref, ab_ref, ub_ref, *, seq):
    seg = seq // SUBLANES
    dirs = ((af_ref, uf_ref), (ab_ref, ub_ref))

    pad_ref[0:A_HALO, :] = jnp.zeros((A_HALO, D_A), f32)
    pad_ref[seq + A_HALO:seq + 2 * A_HALO, :] = jnp.zeros((A_HALO, D_A), f32)

    def copy_in(c, carry):
        r0 = pl.multiple_of(c * A_RC, A_RC)
        pad_ref[pl.ds(r0 + A_HALO, A_RC), :] = xa_ref[pl.ds(r0, A_RC), :]
        return carry

    lax.fori_loop(0, seq // A_RC, copy_in, 0)

    lam = lam_ref[...]
    neg_c_softplus = -LRU_C * (jnp.maximum(-lam, 0.0) + jnp.log1p(jnp.exp(-jnp.abs(lam))))

    def gate_pass(c, carry):
        r0 = pl.multiple_of(c * A_RC, A_RC)
        for n in range(LRU_BLOCKS):
            ln = slice(n * LANES, (n + 1) * LANES)
            win = pad_ref[pl.ds(r0, A_RC + 2 * A_HALO), ln]
            xac = jnp.broadcast_to(cb_ref[:, ln], (A_RC, LANES))
            for k in range(CONV_A_W):
                off = A_HALO - 2 + k
                xac = xac + cw_ref[k:k + 1, ln] * win[off:off + A_RC, :]
            gates = jnp.dot(xac.astype(bf16), wg_ref[n], preferred_element_type=f32) + bg_ref[n]
            for d, (a_ref, u_ref) in enumerate(dirs):
                ga = gates[:, d * LANES:(d + 1) * LANES]
                gx = gates[:, (2 + d) * LANES:(3 + d) * LANES]
                log_a = neg_c_softplus[d:d + 1, ln] * _sigmoid(ga)
                a = jnp.exp(log_a)
                u = jnp.sqrt(jnp.tanh(-log_a) * (1.0 + a * a)) * _sigmoid(gx) * xac
                a_ref[n, pl.ds(r0, A_RC), :] = a
                u_ref[n, pl.ds(r0, A_RC), :] = u
        return carry

    lax.fori_loop(0, seq // A_RC, gate_pass, 0)

    half = LRU_BLOCKS // 2
    for blocks in (range(0, half), range(half, LRU_BLOCKS)):
        blocks = tuple(blocks)

        def scan_step(i, carry, blocks=blocks):
            hf, pf, hb, pb = carry
            ib = seg - 1 - i
            nhf, npf, nhb, npb = [], [], [], []
            for q, n in enumerate(blocks):
                sf = pl.ds(i, SUBLANES, stride=seg)
                sb = pl.ds(ib, SUBLANES, stride=seg)
                a = af_ref[n, sf, :]
                h = a * hf[q] + uf_ref[n, sf, :]
                p = a * pf[q]
                uf_ref[n, sf, :] = h
                af_ref[n, sf, :] = p
                nhf.append(h)
                npf.append(p)
                a = ab_ref[n, sb, :]
                h = a * hb[q] + ub_ref[n, sb, :]
                p = a * pb[q]
                ub_ref[n, sb, :] = h
                ab_ref[n, sb, :] = p
                nhb.append(h)
                npb.append(p)
            return tuple(nhf), tuple(npf), tuple(nhb), tuple(npb)

        zeros = tuple(jnp.zeros((SUBLANES, LANES), f32) for _ in blocks)
        ones = tuple(jnp.ones((SUBLANES, LANES), f32) for _ in blocks)
        hf, pf, hb, pb = lax.fori_loop(0, seg, scan_step, (zeros, ones, zeros, ones))

        carries_f, carries_b = [], []
        for q, n in enumerate(blocks):
            ln = slice(n * LANES, (n + 1) * LANES)
            c = h0_ref[0:1, ln]
            rows = []
            for j in range(SUBLANES):
                rows.append(c)
                c = pf[q][j:j + 1] * c + hf[q][j:j + 1]
            hl_ref[0:1, ln] = c
            carries_f.append(jnp.concatenate(rows, axis=0))
            c = h0_ref[1:2, ln]
            rows = [None] * SUBLANES
            for j in range(SUBLANES - 1, -1, -1):
                rows[j] = c
                c = pb[q][j:j + 1] * c + hb[q][j:j + 1]
            hl_ref[1:2, ln] = c
            carries_b.append(jnp.concatenate(rows, axis=0))

        def fix_step(i, carry, blocks=blocks, carries_f=carries_f, carries_b=carries_b):
            s = pl.ds(i, SUBLANES, stride=seg)
            for q, n in enumerate(blocks):
                uf_ref[n, s, :] = (uf_ref[n, s, :] + af_ref[n, s, :] * carries_f[q]
                                   + ub_ref[n, s, :] + ab_ref[n, s, :] * carries_b[q])
            return carry

        lax.fori_loop(0, seg, fix_step, 0)

    gn = gn_ref[...]

    def out_pass(c, carry):
        r0 = pl.multiple_of(c * A_OC, A_OC)
        outs = []
        ss = jnp.zeros((A_OC, 1), f32)
        for n in range(LRU_BLOCKS):
            ln = slice(n * LANES, (n + 1) * LANES)
            o = uf_ref[n, pl.ds(r0, A_OC), :] * _gelu_tanh(ya_ref[pl.ds(r0, A_OC), ln])
            ss = ss + jnp.sum(o * o, axis=-1, keepdims=True)
            outs.append(o)
        inv = lax.rsqrt(ss * (1.0 / D_A) + EPS)
        for n in range(LRU_BLOCKS):
            ln = slice(n * LANES, (n + 1) * LANES)
            ma_ref[pl.ds(r0, A_OC), ln] = (outs[n] * inv * gn[:, ln]).astype(bf16)
        return carry

    lax.fori_loop(0, seq // A_OC, out_pass, 0)


def _mixa_call(u, h0, cw, cb, wg, bg, lam, gn, batch, seq):
    rows = batch * seq
    small = lambda shape: pl.BlockSpec(shape, lambda b: (0,) * len(shape))
    return pl.pallas_call(
        functools.partial(_mixa_body, seq=seq),
        grid=(batch,),
        in_specs=[
            pl.BlockSpec((seq, D_A), lambda b: (b, 0)),
            pl.BlockSpec((seq, D_A), lambda b: (b, 1)),
            pl.BlockSpec((None, 2, D_A), lambda b: (b, 0, 0)),
            small((CONV_A_W, D_A)),
            small((1, D_A)),
            small((LRU_BLOCKS, LRU_BW, 4 * LRU_BW)),
            small((LRU_BLOCKS, 1, 4 * LRU_BW)),
            small((2, D_A)),
            small((1, D_A)),
        ],
        out_specs=[
            pl.BlockSpec((seq, D_A), lambda b: (b, 0)),
            pl.BlockSpec((None, 2, D_A), lambda b: (b, 0, 0)),
        ],
        out_shape=[jax.ShapeDtypeStruct((rows, D_A), bf16), jax.ShapeDtypeStruct((batch, 2, D_A), f32)],
        scratch_shapes=[pltpu.VMEM((seq + 2 * A_HALO, D_A), f32)]
        + [pltpu.VMEM((LRU_BLOCKS, seq, LANES), f32) for _ in range(4)],
        compiler_params=_cparams(("arbitrary",), 52),
        name="mix_lru",
    )(u, u, h0, cw, cb, wg, bg, lam, gn)


B_RC = 256


def _mixb_body(xb_ref, cw_ref, ds_ref, gn_ref, mb_ref, t_ref, *, seq):
    for g in range(FNET_GROUPS):
        ln = slice(g * FNET_GW, (g + 1) * FNET_GW)
        y = jnp.dot(xb_ref[:, ln].astype(bf16), cw_ref[...], preferred_element_type=f32)
        t_ref[0:seq, ln] = y[:, :FNET_GW].astype(bf16)
        t_ref[seq:2 * seq, ln] = y[:, FNET_GW:].astype(bf16)
    gn = gn_ref[...]
    for c in range(seq // B_RC):
        rows = slice(c * B_RC, (c + 1) * B_RC)
        o = jnp.dot(ds_ref[rows, :], t_ref[...], preferred_element_type=f32)
        mb_ref[rows, :] = _rms_scale(o, gn).astype(bf16)


def _mixb_call(u, cw, ds, gn, batch, seq):
    rows = batch * seq
    small = lambda shape: pl.BlockSpec(shape, lambda b: (0,) * len(shape))
    return pl.pallas_call(
        functools.partial(_mixb_body, seq=seq),
        grid=(batch,),
        in_specs=[
            pl.BlockSpec((seq, D_B), lambda b: (b, 2 * D_A // D_B)),
            small((FNET_GW, 2 * FNET_GW)),
            small((seq, 2 * seq)),
            small((1, D_B)),
        ],
        out_specs=pl.BlockSpec((seq, D_B), lambda b: (b, 0)),
        out_shape=jax.ShapeDtypeStruct((rows, D_B), bf16),
        scratch_shapes=[pltpu.VMEM((2 * seq, D_B), bf16)],
        compiler_params=_cparams(("arbitrary",), 40),
        name="mix_fourier",
    )(u, cw, ds, gn)


C_RC = 32
C_HALO = 16
C_HALF = (CONV_C_W - 1) // 2


def _mixc_body(xv_ref, xg_ref, w_ref, b_ref, lg_ref, lb_ref, gn_ref, mc_ref, pad_ref, *, seq):
    pad_ref[0:C_HALO, :] = jnp.zeros((C_HALO, D_C), f32)
    pad_ref[seq + C_HALO:seq + 2 * C_HALO, :] = jnp.zeros((C_HALO, D_C), f32)

    def glu(c, carry):
        r0 = pl.multiple_of(c * C_RC, C_RC)
        pad_ref[pl.ds(r0 + C_HALO, C_RC), :] = xv_ref[pl.ds(r0, C_RC), :] * _sigmoid(xg_ref[pl.ds(r0, C_RC), :])
        return carry

    lax.fori_loop(0, seq // C_RC, glu, 0)

    lg, lb, gn = lg_ref[...], lb_ref[...], gn_ref[...]
    nblk = D_C // LANES

    def conv(c, carry):
        r0 = pl.multiple_of(c * C_RC, C_RC)
        ys = []
        for q in range(nblk):
            ln = slice(q * LANES, (q + 1) * LANES)
            win = pad_ref[pl.ds(r0, C_RC + 2 * C_HALO), ln]
            acc = jnp.broadcast_to(b_ref[:, ln], (C_RC, LANES))
            for k in range(CONV_C_W):
                off = C_HALO - C_HALF + k
                acc = acc + w_ref[k:k + 1, ln] * win[off:off + C_RC, :]
            ys.append(acc)
        mu = sum(jnp.sum(y, axis=-1, keepdims=True) for y in ys) * (1.0 / D_C)
        ds_ = [y - mu for y in ys]
        var = sum(jnp.sum(d * d, axis=-1, keepdims=True) for d in ds_) * (1.0 / D_C)
        inv = lax.rsqrt(var + EPS)
        outs = []
        for q in range(nblk):
            ln = slice(q * LANES, (q + 1) * LANES)
            v = ds_[q] * inv * lg[:, ln] + lb[:, ln]
            outs.append(v * _sigmoid(v))
        ss = sum(jnp.sum(o * o, axis=-1, keepdims=True) for o in outs) * (1.0 / D_C)
        inv2 = lax.rsqrt(ss + EPS)
        for q in range(nblk):
            ln = slice(q * LANES, (q + 1) * LANES)
            mc_ref[pl.ds(r0, C_RC), ln] = (outs[q] * inv2 * gn[:, ln]).astype(bf16)
        return carry

    lax.fori_loop(0, seq // C_RC, conv, 0)


def _mixc_call(u, w, b, lg, lb, gn, batch, seq):
    rows = batch * seq
    small = lambda shape: pl.BlockSpec(shape, lambda i: (0,) * len(shape))
    col0 = (2 * D_A + D_B) // D_C
    return pl.pallas_call(
        functools.partial(_mixc_body, seq=seq),
        grid=(batch,),
        in_specs=[
            pl.BlockSpec((seq, D_C), lambda i: (i, col0)),
            pl.BlockSpec((seq, D_C), lambda i: (i, col0 + 1)),
            small((CONV_C_W, D_C)),
            small((1, D_C)),
            small((1, D_C)),
            small((1, D_C)),
            small((1, D_C)),
        ],
        out_specs=pl.BlockSpec((seq, D_C), lambda i: (i, 0)),
        out_shape=jax.ShapeDtypeStruct((rows, D_C), bf16),
        scratch_shapes=[pltpu.VMEM((seq + 2 * C_HALO, D_C), f32)],
        compiler_params=_cparams(("arbitrary",), 32),
        name="mix_conv",
    )(u, u, w, b, lg, lb, gn)


OUT_TM = 512


def _outproj_body(x_ref, ma_ref, mb_ref, mc_ref, wo_ref, g_ref, mod_ref, x1_ref, h2_ref):
    acc = jnp.dot(ma_ref[...], wo_ref[0:D_A, :], preferred_element_type=f32)
    acc = acc + jnp.dot(mb_ref[...], wo_ref[D_A:D_A + D_B, :], preferred_element_type=f32)
    acc = acc + jnp.dot(mc_ref[...], wo_ref[D_A + D_B:, :], preferred_element_type=f32)
    x1 = x_ref[...] + mod_ref[GATE1:GATE1 + 1, :] * acc
    x1_ref[...] = x1
    h2_ref[...] = _normmod(x1, g_ref[...], mod_ref[SCALE2:SCALE2 + 1, :], mod_ref[SHIFT2:SHIFT2 + 1, :]).astype(bf16)


def _outproj_call(x, ma, mb, mc, wo, g, mod, seq):
    rows = x.shape[0]
    row = lambda w: pl.BlockSpec((OUT_TM, w), lambda i: (i, 0))
    return pl.pallas_call(
        _outproj_body,
        grid=(rows // OUT_TM,),
        in_specs=[
            row(D_MODEL), row(D_A), row(D_B), row(D_C),
            pl.BlockSpec((D_MODEL, D_MODEL), lambda i: (0, 0)),
            pl.BlockSpec((1, D_MODEL), lambda i: (0, 0)),
            _mod_spec(OUT_TM, seq, mod.shape[0]),
        ],
        out_specs=[row(D_MODEL), row(D_MODEL)],
        out_shape=[jax.ShapeDtypeStruct((rows, D_MODEL), f32), jax.ShapeDtypeStruct((rows, D_MODEL), bf16)],
        compiler_params=_cparams(("arbitrary",), 52),
        name="out_proj",
    )(x, ma, mb, mc, wo, g, mod)


FFN_TM = 512
FFN_TF = 512


def _ffn_body(h2_ref, x1_ref, wg_ref, wu_ref, wd_ref, mod_ref, gn_ref, modn_ref, *rest, final):
    if final:
        y_ref, acc_ref = rest
    else:
        x2_ref, hn_ref, acc_ref = rest
    f = pl.program_id(1)

    @pl.when(f == 0)
    def _():
        acc_ref[...] = jnp.zeros_like(acc_ref)

    h = h2_ref[...]
    g = jnp.dot(h, wg_ref[...], preferred_element_type=f32)
    u = jnp.dot(h, wu_ref[...], preferred_element_type=f32)
    act = (g * _sigmoid(g) * u).astype(bf16)
    acc_ref[...] += jnp.dot(act, wd_ref[...], preferred_element_type=f32)

    @pl.when(f == pl.num_programs(1) - 1)
    def _():
        x2 = x1_ref[...] + mod_ref[GATE2:GATE2 + 1, :] * acc_ref[...]
        if final:
            y_ref[...] = _rms_scale(x2, gn_ref[...])
        else:
            x2_ref[...] = x2
            hn_ref[...] = _normmod(x2, gn_ref[...], modn_ref[SCALE1:SCALE1 + 1, :],
                                   modn_ref[SHIFT1:SHIFT1 + 1, :]).astype(bf16)


def _ffn_call(h2, x1, wgu, wd, mod, gn, modn, seq, final):
    rows = h2.shape[0]
    nf = D_FF // FFN_TF
    row = pl.BlockSpec((FFN_TM, D_MODEL), lambda i, f: (i, 0))
    if final:
        out_specs = [row]
        out_shape = [jax.ShapeDtypeStruct((rows, D_MODEL), f32)]
    else:
        out_specs = [row, row]
        out_shape = [jax.ShapeDtypeStruct((rows, D_MODEL), f32), jax.ShapeDtypeStruct((rows, D_MODEL), bf16)]
    return pl.pallas_call(
        functools.partial(_ffn_body, final=final),
        grid=(rows // FFN_TM, nf),
        in_specs=[
            row, row,
            pl.BlockSpec((D_MODEL, FFN_TF), lambda i, f: (0, f)),
            pl.BlockSpec((D_MODEL, FFN_TF), lambda i, f: (0, nf + f)),
            pl.BlockSpec((FFN_TF, D_MODEL), lambda i, f: (f, 0)),
            _mod_spec(FFN_TM, seq, mod.shape[0]),
            pl.BlockSpec((1, D_MODEL), lambda i, f: (0, 0)),
            _mod_spec(FFN_TM, seq, modn.shape[0]),
        ],
        out_specs=out_specs,
        out_shape=out_shape,
        scratch_shapes=[pltpu.VMEM((FFN_TM, D_MODEL), f32)],
        compiler_params=_cparams(("arbitrary", "arbitrary"), 52),
        name="ffn_final" if final else "ffn",
    )(h2, x1, wgu, wgu, wd, mod, gn, modn)


def _dft_cos_sin(n):
    k = np.arange(n, dtype=np.int64)
    ang = 2.0 * np.pi * ((k[:, None] * k[None, :]) % n).astype(np.float64) / n
    return np.cos(ang), np.sin(ang)


def _fourier_consts(seq):
    cw, sw = _dft_cos_sin(FNET_GW)
    cs, ss = _dft_cos_sin(seq)
    scale = 1.0 / np.sqrt(float(seq * FNET_GW))
    chan = np.concatenate([cw, sw], axis=1).astype(np.float32)
    pos = (np.concatenate([cs, -ss], axis=1) * scale).astype(np.float32)
    return jnp.asarray(chan).astype(bf16), jnp.asarray(pos).astype(bf16)


def _path(x, h, batch, seq, mods, h0s, p):
    depth = len(mods)
    chan_dft, pos_dft = _fourier_consts(seq)
    states = []
    for l in range(depth):
        u = _inproj_call(h, p["w_in"][l])
        ma, hl = _mixa_call(u, h0s[l], p["conv_a_w"][l], p["conv_a_b"][l][None], p["wg"][l], p["bg"][l],
                            p["lru_lam"][l], p["out_norm"][l][None, :D_A], batch, seq)
        mb = _mixb_call(u, chan_dft, pos_dft, p["out_norm"][l][None, D_A:D_A + D_B], batch, seq)
        mc = _mixc_call(u, p["conv_c_w"][l], p["conv_c_b"][l][None], p["ln_c_g"][l][None], p["ln_c_b"][l][None],
                        p["out_norm"][l][None, D_A + D_B:], batch, seq)
        x1, h2 = _outproj_call(x, ma, mb, mc, p["w_out"][l], p["norm_ffn"][l][None], mods[l], seq)
        states.append(hl)
        if l + 1 < depth:
            x, h = _ffn_call(h2, x1, p["w_gu"][l], p["w_down"][l], mods[l], p["norm_mix"][l + 1][None],
                             mods[l + 1], seq, final=False)
        else:
            (y,) = _ffn_call(h2, x1, p["w_gu"][l], p["w_down"][l], mods[l], p["final_norm"][None],
                             mods[l], seq, final=True)
    return y, states


def kernel(x_prompt, x_sample, c, state_lru, c_ctx, w_mod, b_mod, norm_mix, norm_ffn, w_in, conv_a_w, conv_a_b,
           lru_wa, lru_ba, lru_wx, lru_bx, lru_lam, conv_c_w, conv_c_b, ln_c_g, ln_c_b, out_norm, w_out, w_gu,
           w_down, final_norm):
    depth = w_in.shape[0]
    bp, sp, _ = x_prompt.shape
    bs, ss, _ = x_sample.shape
    assert bs + 1 <= SUBLANES

    cv = jnp.concatenate([c_ctx[None], c, jnp.zeros((SUBLANES - 1 - bs, D_MODEL), f32)], axis=0)
    mod = _mod_call(cv, w_mod, b_mod).reshape(depth, SUBLANES, N_MOD, D_MODEL)
    mods_ctx = [mod[l, 0:1] for l in range(depth)]
    mods_lat = [mod[l, 1:1 + bs] for l in range(depth)]

    wg = jnp.concatenate([lru_wa[:, 0], lru_wa[:, 1], lru_wx[:, 0], lru_wx[:, 1]], axis=-1).astype(bf16)
    blk = lambda v: v.reshape(depth, LRU_BLOCKS, 1, LRU_BW)
    bg = jnp.concatenate([blk(lru_ba[:, 0]), blk(lru_ba[:, 1]), blk(lru_bx[:, 0]), blk(lru_bx[:, 1])], axis=-1)
    p = dict(w_in=w_in.astype(bf16), w_out=w_out.astype(bf16), w_gu=w_gu.astype(bf16), w_down=w_down.astype(bf16),
             conv_a_w=conv_a_w, conv_a_b=conv_a_b, wg=wg, bg=bg, lru_lam=lru_lam, conv_c_w=conv_c_w,
             conv_c_b=conv_c_b, ln_c_g=ln_c_g, ln_c_b=ln_c_b, out_norm=out_norm, norm_ffn=norm_ffn,
             norm_mix=norm_mix, final_norm=final_norm)

    xp = x_prompt.reshape(bp * sp, D_MODEL)
    hp = _pre_ctx_call(xp, norm_mix[0][None], mods_ctx[0], sp)
    h0_ctx = jnp.zeros((bp, 2, D_A), f32)
    y_p, states = _path(xp, hp, bp, sp, mods_ctx, [h0_ctx] * depth, p)

    nf = D_MODEL // 4
    omega = (1.0 / (10000.0 ** (jnp.arange(nf, dtype=f32) / nf)))[None]
    xs, hs = _pre_lat_call(x_sample.reshape(bs * ss, D_MODEL), omega, norm_mix[0][None], mods_lat[0], ss)
    y_s, _ = _path(xs, hs, bs, ss, mods_lat, [state_lru[:, l] for l in range(depth)], p)

    return (y_p.reshape(bp, sp, D_MODEL), y_s.reshape(bs, ss, D_MODEL), jnp.stack(states, axis=1))
```

```python
import functools

import numpy as np
import jax
import jax.numpy as jnp
from jax import lax
from jax.experimental import pallas as pl
from jax.experimental.pallas import tpu as pltpu

f32 = jnp.float32
bf16 = jnp.bfloat16

D_MODEL = 2048
D_A = 1024
D_B = 512
D_C = 512
LRU_BLOCKS = 8
LRU_BW = 128
LRU_C = 8.0
CONV_A_W = 4
FNET_GROUPS = 4
FNET_GW = 128
CONV_C_W = 31
D_IN = 2 * D_A + D_B + 2 * D_C
D_FF = 5632
N_MOD = 6
GRID_W = 64
EPS = 1e-6

LANES = 128
SUBLANES = 8
MIB = 1024 * 1024

SHIFT1, SCALE1, GATE1, SHIFT2, SCALE2, GATE2 = range(N_MOD)


def _cparams(sem, vmem_mib):
    return pltpu.CompilerParams(dimension_semantics=sem, vmem_limit_bytes=vmem_mib * MIB)


def _sigmoid(x):
    return 0.5 * jnp.tanh(0.5 * x) + 0.5


def _rms_scale(x, g):
    ms = jnp.mean(x * x, axis=-1, keepdims=True)
    return x * lax.rsqrt(ms + EPS) * g


def _normmod(x, g, scale, shift):
    return _rms_scale(x, g) * (1.0 + scale) + shift


def _gelu_tanh(x):
    return x * (0.5 * (1.0 + jnp.tanh(np.sqrt(2.0 / np.pi).astype(np.float32) * (x + 0.044715 * (x * x * x)))))


def _shift_down(x):
    row = lax.broadcasted_iota(jnp.int32, x.shape, 0)
    return jnp.where(row == 0, 0.0, pltpu.roll(x, 1, 0))


def _shift_up(x):
    row = lax.broadcasted_iota(jnp.int32, x.shape, 0)
    return jnp.where(row == SUBLANES - 1, 0.0, pltpu.roll(x, SUBLANES - 1, 0))


MOD_TN = 1024


def _mod_body(cv_ref, w_ref, b_ref, o_ref):
    cv = cv_ref[...]
    s = (cv * _sigmoid(cv)).astype(bf16)
    o_ref[...] = jnp.dot(s, w_ref[...].astype(bf16), preferred_element_type=f32) + b_ref[...]


def _mod_call(cv, w_mod, b_mod):
    depth = w_mod.shape[0]
    n = N_MOD * D_MODEL
    return pl.pallas_call(
        _mod_body,
        grid=(depth, n // MOD_TN),
        in_specs=[
            pl.BlockSpec((SUBLANES, D_MODEL), lambda l, j: (0, 0)),
            pl.BlockSpec((None, D_MODEL, MOD_TN), lambda l, j: (l, 0, j)),
            pl.BlockSpec((None, 1, MOD_TN), lambda l, j: (l, 0, j)),
        ],
        out_specs=pl.BlockSpec((None, SUBLANES, MOD_TN), lambda l, j: (l, 0, j)),
        out_shape=jax.ShapeDtypeStruct((depth, SUBLANES, n), f32),
        compiler_params=_cparams(("arbitrary", "arbitrary"), 40),
        name="mod_vectors",
    )(cv, w_mod, b_mod.reshape(depth, 1, n))


PRE_TM = 512


def _mod_spec(tm, seq, nmod):
    if nmod == 1:
        return pl.BlockSpec((None, N_MOD, D_MODEL), lambda i, *_: (0, 0, 0))
    return pl.BlockSpec((None, N_MOD, D_MODEL), lambda i, *_: ((i * tm) // seq, 0, 0))


def _pre_ctx_body(x_ref, g_ref, mod_ref, h_ref):
    h_ref[...] = _normmod(x_ref[...], g_ref[...], mod_ref[SCALE1:SCALE1 + 1, :],
                          mod_ref[SHIFT1:SHIFT1 + 1, :]).astype(bf16)


def _pre_ctx_call(x, g, mod, seq):
    rows = x.shape[0]
    assert rows % PRE_TM == 0
    return pl.pallas_call(
        _pre_ctx_body,
        grid=(rows // PRE_TM,),
        in_specs=[
            pl.BlockSpec((PRE_TM, D_MODEL), lambda i: (i, 0)),
            pl.BlockSpec((1, D_MODEL), lambda i: (0, 0)),
            _mod_spec(PRE_TM, seq, mod.shape[0]),
        ],
        out_specs=pl.BlockSpec((PRE_TM, D_MODEL), lambda i: (i, 0)),
        out_shape=jax.ShapeDtypeStruct((rows, D_MODEL), bf16),
        compiler_params=_cparams(("arbitrary",), 32),
        name="pre_ctx",
    )(x, g, mod)


def _pre_lat_body(x_ref, om_ref, g_ref, mod_ref, x0_ref, h_ref, *, tiles_per_seq):
    nf = D_MODEL // 4
    grid_rows = PRE_TM // GRID_W
    i = pl.program_id(0)
    r_base = (i % tiles_per_seq) * grid_rows
    om = om_ref[...]
    ang_r = (r_base + lax.broadcasted_iota(jnp.int32, (grid_rows, nf), 0)).astype(f32) * om
    sin_r, cos_r = jnp.sin(ang_r), jnp.cos(ang_r)
    ang_c = lax.broadcasted_iota(jnp.int32, (GRID_W, nf), 0).astype(f32) * om
    sin_c, cos_c = jnp.sin(ang_c), jnp.cos(ang_c)
    g = g_ref[...]
    scale = mod_ref[SCALE1:SCALE1 + 1, :]
    shift = mod_ref[SHIFT1:SHIFT1 + 1, :]
    for q in range(grid_rows):
        rows = slice(q * GRID_W, (q + 1) * GRID_W)
        pos = jnp.concatenate([jnp.broadcast_to(sin_r[q:q + 1], (GRID_W, nf)),
                               jnp.broadcast_to(cos_r[q:q + 1], (GRID_W, nf)), sin_c, cos_c], axis=-1)
        x0 = x_ref[rows, :] + pos
        x0_ref[rows, :] = x0
        h_ref[rows, :] = _normmod(x0, g, scale, shift).astype(bf16)


def _pre_lat_call(x, omega, g, mod, seq):
    rows = x.shape[0]
    assert seq % PRE_TM == 0 and PRE_TM % GRID_W == 0 and rows % PRE_TM == 0
    row_spec = pl.BlockSpec((PRE_TM, D_MODEL), lambda i: (i, 0))
    return pl.pallas_call(
        functools.partial(_pre_lat_body, tiles_per_seq=seq // PRE_TM),
        grid=(rows // PRE_TM,),
        in_specs=[
            row_spec,
            pl.BlockSpec((1, D_MODEL // 4), lambda i: (0, 0)),
            pl.BlockSpec((1, D_MODEL), lambda i: (0, 0)),
            _mod_spec(PRE_TM, seq, mod.shape[0]),
        ],
        out_specs=[row_spec, row_spec],
        out_shape=[jax.ShapeDtypeStruct((rows, D_MODEL), f32), jax.ShapeDtypeStruct((rows, D_MODEL), bf16)],
        compiler_params=_cparams(("arbitrary",), 40),
        name="pre_lat",
    )(x, omega, g, mod)


INPROJ_TM = 1024
INPROJ_TN = 512


def _inproj_body(h_ref, w_ref, o_ref):
    o_ref[...] = jnp.dot(h_ref[...], w_ref[...], preferred_element_type=f32)


def _inproj_call(h, w, layer):
    rows = h.shape[0]
    assert rows % INPROJ_TM == 0
    return pl.pallas_call(
        _inproj_body,
        grid=(rows // INPROJ_TM, D_IN // INPROJ_TN),
        in_specs=[
            pl.BlockSpec((INPROJ_TM, D_MODEL), lambda i, j: (i, 0)),
            pl.BlockSpec((None, D_MODEL, INPROJ_TN), lambda i, j: (layer, 0, j)),
        ],
        out_specs=pl.BlockSpec((INPROJ_TM, INPROJ_TN), lambda i, j: (i, j)),
        out_shape=jax.ShapeDtypeStruct((rows, D_IN), f32),
        compiler_params=_cparams(("arbitrary", "arbitrary"), 40),
        name="in_proj",
    )(h, w)


SEG_CH = 32
A_GM = 256
A_GS = A_GM // SUBLANES
A_ES = 8
A_EC = A_ES * SUBLANES
A_OC = 128
SQRT_FLOOR = 1e-37


def _mixa_body(xa_ref, ya_ref, h0_ref, cw_ref, cb_ref, wg_ref, bg_ref, lam_ref, gn_ref,
               ma_ref, hl_ref, xe_ref, af_ref, uf_ref, ab_ref, ub_ref, *, seq):
    seg = seq // SUBLANES
    dirs = ((af_ref, uf_ref), (ab_ref, ub_ref))
    tile = lambda: jnp.zeros((SUBLANES, LANES), f32)

    for n in range(LRU_BLOCKS):
        ln = slice(n * LANES, (n + 1) * LANES)
        for i0 in range(0, seg, SEG_CH):
            v = jnp.stack([xa_ref[j * seg + i0:j * seg + i0 + SEG_CH, ln] for j in range(SUBLANES)], axis=0)
            xe_ref[n, 2 + i0:2 + i0 + SEG_CH] = jnp.swapaxes(v, 0, 1)
        xe_ref[n, 0] = _shift_down(xe_ref[n, seg])
        xe_ref[n, 1] = _shift_down(xe_ref[n, seg + 1])
        xe_ref[n, seg + 2] = _shift_up(xe_ref[n, 2])

    lam = lam_ref[...]
    half_rate = (-0.5 * LRU_C) * (jnp.maximum(-lam, 0.0) + jnp.log1p(jnp.exp(-jnp.abs(lam))))

    def gate_chunk(c, state):
        base = c * A_GS
        new_state = []
        for n in range(LRU_BLOCKS):
            ln = slice(n * LANES, (n + 1) * LANES)
            xac = jnp.broadcast_to(cb_ref[:, ln], (A_GS, SUBLANES, LANES))
            for k in range(CONV_A_W):
                xac = xac + cw_ref[k:k + 1, ln] * xe_ref[n, pl.ds(base + k, A_GS)]
            gates = jnp.dot(xac.reshape(A_GM, LANES).astype(bf16), wg_ref[n],
                            preferred_element_type=f32) + bg_ref[n]
            hf, pf, hb, pb = state[n]
            for s in range(A_GS // A_ES):
                rows = slice(s * A_EC, (s + 1) * A_EC)
                steps = slice(s * A_ES, (s + 1) * A_ES)
                xh = 0.5 * xac[steps]
                au = []
                for d, (a_ref, u_ref) in enumerate(dirs):
                    ga = gates[rows, d * LANES:(d + 1) * LANES]
                    gx = gates[rows, (2 + d) * LANES:(3 + d) * LANES]
                    rate = half_rate[d:d + 1, ln]
                    log_a = rate * jnp.tanh(ga) + rate
                    a = jnp.exp(log_a)
                    var = (-1.0 - a * a) * jnp.tanh(log_a)
                    std = var * lax.rsqrt(jnp.maximum(var, SQRT_FLOOR))
                    a = a.reshape(A_ES, SUBLANES, LANES)
                    u = (std * (1.0 + jnp.tanh(gx))).reshape(A_ES, SUBLANES, LANES) * xh
                    a_ref[n, pl.ds(base + s * A_ES, A_ES)] = a
                    u_ref[n, pl.ds(base + s * A_ES, A_ES)] = u
                    au.append((a, u))
                for i in range(A_ES):
                    a, u = au[0]
                    hf = a[i] * hf + u[i]
                    pf = a[i] * pf
                    a, u = au[1]
                    hb = hb + pb * u[i]
                    pb = pb * a[i]
            new_state.append((hf, pf, hb, pb))
        return tuple(new_state)

    init = tuple((tile(), tile() + 1.0, tile(), tile() + 1.0) for _ in range(LRU_BLOCKS))
    ends = lax.fori_loop(0, seq // A_GM, gate_chunk, init)

    carries_f, carries_b = [], []
    for n in range(LRU_BLOCKS):
        ln = slice(n * LANES, (n + 1) * LANES)
        hf, pf, hb, pb = ends[n]
        c = h0_ref[0:1, ln]
        rows = []
        for j in range(SUBLANES):
            rows.append(c)
            c = pf[j:j + 1] * c + hf[j:j + 1]
        hl_ref[0:1, ln] = c
        carries_f.append(jnp.concatenate(rows, axis=0))
        c = h0_ref[1:2, ln]
        rows = [None] * SUBLANES
        for j in range(SUBLANES - 1, -1, -1):
            rows[j] = c
            c = pb[j:j + 1] * c + hb[j:j + 1]
        hl_ref[1:2, ln] = c
        carries_b.append(jnp.concatenate(rows, axis=0))

    hf, hb = carries_f, carries_b
    for i in range(seg):
        ib = seg - 1 - i
        for n in range(LRU_BLOCKS):
            hf[n] = af_ref[n, i] * hf[n] + uf_ref[n, i]
            uf_ref[n, i] = hf[n]
            hb[n] = ab_ref[n, ib] * hb[n] + ub_ref[n, ib]
            ub_ref[n, ib] = hb[n]

    for n in range(LRU_BLOCKS):
        for i0 in range(0, seg, SEG_CH):
            w = jnp.swapaxes(uf_ref[n, i0:i0 + SEG_CH] + ub_ref[n, i0:i0 + SEG_CH], 0, 1)
            for j in range(SUBLANES):
                t0 = (j * seg + i0) // SUBLANES
                af_ref[n, t0:t0 + SEG_CH // SUBLANES] = w[j].reshape(SEG_CH // SUBLANES, SUBLANES, LANES)

    gn = gn_ref[...]

    def out_pass(c, carry):
        r0 = pl.multiple_of(c * A_OC, A_OC)
        t0 = c * (A_OC // SUBLANES)
        outs = []
        sq = jnp.zeros((A_OC, LANES), f32)
        for n in range(LRU_BLOCKS):
            ln = slice(n * LANES, (n + 1) * LANES)
            h = af_ref[n, pl.ds(t0, A_OC // SUBLANES)].reshape(A_OC, LANES)
            o = h * _gelu_tanh(ya_ref[pl.ds(r0, A_OC), ln])
            sq = sq + o * o
            outs.append(o)
        inv = lax.rsqrt(jnp.sum(sq, axis=-1, keepdims=True) * (1.0 / D_A) + EPS)
        for n in range(LRU_BLOCKS):
            ln = slice(n * LANES, (n + 1) * LANES)
            ma_ref[pl.ds(r0, A_OC), ln] = (outs[n] * inv * gn[:, ln]).astype(bf16)
        return carry

    lax.fori_loop(0, seq // A_OC, out_pass, 0)


def _mixa_call(u, h0, cw, cb, wg, bg, lam, gn, batch, seq):
    rows = batch * seq
    seg = seq // SUBLANES
    assert seq % A_GM == 0 and seg % SEG_CH == 0
    small = lambda shape: pl.BlockSpec(shape, lambda b: (0,) * len(shape))
    seg_buf = pltpu.VMEM((LRU_BLOCKS, seg, SUBLANES, LANES), f32)
    return pl.pallas_call(
        functools.partial(_mixa_body, seq=seq),
        grid=(batch,),
        in_specs=[
            pl.BlockSpec((seq, D_A), lambda b: (b, 0)),
            pl.BlockSpec((seq, D_A), lambda b: (b, 1)),
            pl.BlockSpec((None, 2, D_A), lambda b: (b, 0, 0)),
            small((CONV_A_W, D_A)),
            small((1, D_A)),
            small((LRU_BLOCKS, LRU_BW, 4 * LRU_BW)),
            small((LRU_BLOCKS, 1, 4 * LRU_BW)),
            small((2, D_A)),
            small((1, D_A)),
        ],
        out_specs=[
            pl.BlockSpec((seq, D_A), lambda b: (b, 0)),
            pl.BlockSpec((None, 2, D_A), lambda b: (b, 0, 0)),
        ],
        out_shape=[jax.ShapeDtypeStruct((rows, D_A), bf16), jax.ShapeDtypeStruct((batch, 2, D_A), f32)],
        scratch_shapes=[
            pltpu.VMEM((LRU_BLOCKS, seg + CONV_A_W - 1, SUBLANES, LANES), f32),
            seg_buf, seg_buf, seg_buf, seg_buf,
        ],
        compiler_params=_cparams(("arbitrary",), 52),
        name="mix_lru",
    )(u, u, h0, cw, cb, wg, bg, lam, gn)


B_RC = 256


def _mixb_body(xb_ref, cw_ref, ds_ref, gn_ref, mb_ref, t_ref, *, seq):
    for g in range(FNET_GROUPS):
        ln = slice(g * FNET_GW, (g + 1) * FNET_GW)
        y = jnp.dot(xb_ref[:, ln].astype(bf16), cw_ref[...], preferred_element_type=f32)
        t_ref[0:seq, ln] = y[:, :FNET_GW].astype(bf16)
        t_ref[seq:2 * seq, ln] = y[:, FNET_GW:].astype(bf16)
    gn = gn_ref[...]
    for c in range(seq // B_RC):
        rows = slice(c * B_RC, (c + 1) * B_RC)
        o = jnp.dot(ds_ref[rows, :], t_ref[...], preferred_element_type=f32)
        mb_ref[rows, :] = _rms_scale(o, gn).astype(bf16)


def _mixb_call(u, cw, ds, gn, batch, seq):
    rows = batch * seq
    assert seq % B_RC == 0
    small = lambda shape: pl.BlockSpec(shape, lambda b: (0,) * len(shape))
    return pl.pallas_call(
        functools.partial(_mixb_body, seq=seq),
        grid=(batch,),
        in_specs=[
            pl.BlockSpec((seq, D_B), lambda b: (b, 2 * D_A // D_B)),
            small((FNET_GW, 2 * FNET_GW)),
            small((seq, 2 * seq)),
            small((1, D_B)),
        ],
        out_specs=pl.BlockSpec((seq, D_B), lambda b: (b, 0)),
        out_shape=jax.ShapeDtypeStruct((rows, D_B), bf16),
        scratch_shapes=[pltpu.VMEM((2 * seq, D_B), bf16)],
        compiler_params=_cparams(("arbitrary",), 40),
        name="mix_fourier",
    )(u, cw, ds, gn)


C_HALF = (CONV_C_W - 1) // 2
C_RS = 4
C_NS = 16
C_NBLK = D_C // LANES
C_PARTS = 4


def _mixc_body(xv_ref, xg_ref, w_ref, b_ref, lg_ref, lb_ref, gn_ref, mc_ref, ve_ref, wb_ref, op_ref, *, seq):
    seg = seq // SUBLANES
    for q in range(C_NBLK):
        ln = slice(q * LANES, (q + 1) * LANES)
        for i0 in range(0, seg, SEG_CH):
            parts = []
            for j in range(SUBLANES):
                rows = slice(j * seg + i0, j * seg + i0 + SEG_CH)
                parts.append(xv_ref[rows, ln] * _sigmoid(xg_ref[rows, ln]))
            ve_ref[q, C_HALF + i0:C_HALF + i0 + SEG_CH] = jnp.swapaxes(jnp.stack(parts, axis=0), 0, 1)
        for m in range(1, C_HALF + 1):
            ve_ref[q, C_HALF - m] = _shift_down(ve_ref[q, C_HALF + seg - m])
            ve_ref[q, C_HALF + seg - 1 + m] = _shift_up(ve_ref[q, C_HALF + m - 1])
        for k in range(CONV_C_W):
            wb_ref[k, q] = jnp.broadcast_to(w_ref[k:k + 1, ln], (SUBLANES, LANES))

    lg, lb, gn = lg_ref[...], lb_ref[...], gn_ref[...]

    def conv(c, carry):
        base = c * C_RS
        for q in range(C_NBLK):
            ln = slice(q * LANES, (q + 1) * LANES)
            win = ve_ref[q, pl.ds(base, C_RS + CONV_C_W - 1)]
            parts = [jnp.broadcast_to(b_ref[:, ln], (C_RS, SUBLANES, LANES))] + [None] * (C_PARTS - 1)
            for k in range(CONV_C_W):
                term = wb_ref[k, q] * win[k:k + C_RS]
                r = k % C_PARTS
                parts[r] = term if parts[r] is None else parts[r] + term
            op_ref[q, pl.ds(base, C_RS)] = sum(parts[1:], parts[0])
        return carry

    lax.fori_loop(0, seg // C_RS, conv, 0)

    def norms(c, carry):
        base = c * C_NS
        ys = [op_ref[q, pl.ds(base, C_NS)].reshape(C_NS * SUBLANES, LANES) for q in range(C_NBLK)]
        lane_mean = lambda vs: jnp.sum(sum(vs[1:], vs[0]), axis=-1, keepdims=True) * (1.0 / D_C)
        mu = lane_mean(ys)
        ds_ = [y - mu for y in ys]
        inv = lax.rsqrt(lane_mean([d * d for d in ds_]) + EPS)
        outs = []
        for q in range(C_NBLK):
            ln = slice(q * LANES, (q + 1) * LANES)
            v = ds_[q] * inv * lg[:, ln] + lb[:, ln]
            outs.append(v * _sigmoid(v))
        inv2 = lax.rsqrt(lane_mean([o * o for o in outs]) + EPS)
        for q in range(C_NBLK):
            ln = slice(q * LANES, (q + 1) * LANES)
            op_ref[q, pl.ds(base, C_NS)] = (outs[q] * inv2 * gn[:, ln]).reshape(C_NS, SUBLANES, LANES)
        return carry

    lax.fori_loop(0, seg // C_NS, norms, 0)

    for q in range(C_NBLK):
        ln = slice(q * LANES, (q + 1) * LANES)
        for i0 in range(0, seg, SEG_CH):
            w = jnp.swapaxes(op_ref[q, i0:i0 + SEG_CH], 0, 1)
            for j in range(SUBLANES):
                mc_ref[j * seg + i0:j * seg + i0 + SEG_CH, ln] = w[j].astype(bf16)


def _mixc_call(u, w, b, lg, lb, gn, batch, seq):
    rows = batch * seq
    seg = seq // SUBLANES
    assert seg % SEG_CH == 0 and seg >= C_HALF
    small = lambda shape: pl.BlockSpec(shape, lambda i: (0,) * len(shape))
    col0 = (2 * D_A + D_B) // D_C
    return pl.pallas_call(
        functools.partial(_mixc_body, seq=seq),
        grid=(batch,),
        in_specs=[
            pl.BlockSpec((seq, D_C), lambda i: (i, col0)),
            pl.BlockSpec((seq, D_C), lambda i: (i, col0 + 1)),
            small((CONV_C_W, D_C)),
            small((1, D_C)),
            small((1, D_C)),
            small((1, D_C)),
            small((1, D_C)),
        ],
        out_specs=pl.BlockSpec((seq, D_C), lambda i: (i, 0)),
        out_shape=jax.ShapeDtypeStruct((rows, D_C), bf16),
        scratch_shapes=[
            pltpu.VMEM((C_NBLK, seg + CONV_C_W - 1, SUBLANES, LANES), f32),
            pltpu.VMEM((CONV_C_W, C_NBLK, SUBLANES, LANES), f32),
            pltpu.VMEM((C_NBLK, seg, SUBLANES, LANES), f32),
        ],
        compiler_params=_cparams(("arbitrary",), 32),
        name="mix_conv",
    )(u, u, w, b, lg, lb, gn)


OUT_TM = 512


def _outproj_body(x_ref, ma_ref, mb_ref, mc_ref, wo_ref, g_ref, mod_ref, x1_ref, h2_ref):
    acc = jnp.dot(ma_ref[...], wo_ref[0:D_A, :], preferred_element_type=f32)
    acc = acc + jnp.dot(mb_ref[...], wo_ref[D_A:D_A + D_B, :], preferred_element_type=f32)
    acc = acc + jnp.dot(mc_ref[...], wo_ref[D_A + D_B:, :], preferred_element_type=f32)
    x1 = x_ref[...] + mod_ref[GATE1:GATE1 + 1, :] * acc
    x1_ref[...] = x1
    h2_ref[...] = _normmod(x1, g_ref[...], mod_ref[SCALE2:SCALE2 + 1, :], mod_ref[SHIFT2:SHIFT2 + 1, :]).astype(bf16)


def _outproj_call(x, ma, mb, mc, wo, layer, g, mod, seq):
    rows = x.shape[0]
    assert rows % OUT_TM == 0
    row = lambda w: pl.BlockSpec((OUT_TM, w), lambda i: (i, 0))
    return pl.pallas_call(
        _outproj_body,
        grid=(rows // OUT_TM,),
        in_specs=[
            row(D_MODEL), row(D_A), row(D_B), row(D_C),
            pl.BlockSpec((None, D_MODEL, D_MODEL), lambda i: (layer, 0, 0)),
            pl.BlockSpec((1, D_MODEL), lambda i: (0, 0)),
            _mod_spec(OUT_TM, seq, mod.shape[0]),
        ],
        out_specs=[row(D_MODEL), row(D_MODEL)],
        out_shape=[jax.ShapeDtypeStruct((rows, D_MODEL), f32), jax.ShapeDtypeStruct((rows, D_MODEL), bf16)],
        compiler_params=_cparams(("arbitrary",), 52),
        name="out_proj",
    )(x, ma, mb, mc, wo, g, mod)


FFN_TM = 512
FFN_TF = 512


def _ffn_body(h2_ref, x1_ref, wg_ref, wu_ref, wd_ref, mod_ref, gn_ref, modn_ref, *rest, final):
    if final:
        y_ref, acc_ref = rest
    else:
        x2_ref, hn_ref, acc_ref = rest
    f = pl.program_id(1)

    @pl.when(f == 0)
    def _():
        acc_ref[...] = jnp.zeros_like(acc_ref)

    h = h2_ref[...]
    g = jnp.dot(h, wg_ref[...], preferred_element_type=f32)
    u = jnp.dot(h, wu_ref[...], preferred_element_type=f32)
    act = (g * _sigmoid(g) * u).astype(bf16)
    acc_ref[...] += jnp.dot(act, wd_ref[...], preferred_element_type=f32)

    @pl.when(f == pl.num_programs(1) - 1)
    def _():
        x2 = x1_ref[...] + mod_ref[GATE2:GATE2 + 1, :] * acc_ref[...]
        if final:
            y_ref[...] = _rms_scale(x2, gn_ref[...])
        else:
            x2_ref[...] = x2
            hn_ref[...] = _normmod(x2, gn_ref[...], modn_ref[SCALE1:SCALE1 + 1, :],
                                   modn_ref[SHIFT1:SHIFT1 + 1, :]).astype(bf16)


def _ffn_call(h2, x1, wgu, wd, layer, mod, gn, modn, seq, final):
    rows = h2.shape[0]
    assert rows % FFN_TM == 0
    nf = D_FF // FFN_TF
    row = pl.BlockSpec((FFN_TM, D_MODEL), lambda i, f: (i, 0))
    if final:
        out_specs = [row]
        out_shape = [jax.ShapeDtypeStruct((rows, D_MODEL), f32)]
    else:
        out_specs = [row, row]
        out_shape = [jax.ShapeDtypeStruct((rows, D_MODEL), f32), jax.ShapeDtypeStruct((rows, D_MODEL), bf16)]
    return pl.pallas_call(
        functools.partial(_ffn_body, final=final),
        grid=(rows // FFN_TM, nf),
        in_specs=[
            row, row,
            pl.BlockSpec((None, D_MODEL, FFN_TF), lambda i, f: (layer, 0, f)),
            pl.BlockSpec((None, D_MODEL, FFN_TF), lambda i, f: (layer, 0, nf + f)),
            pl.BlockSpec((None, FFN_TF, D_MODEL), lambda i, f: (layer, f, 0)),
            _mod_spec(FFN_TM, seq, mod.shape[0]),
            pl.BlockSpec((1, D_MODEL), lambda i, f: (0, 0)),
            _mod_spec(FFN_TM, seq, modn.shape[0]),
        ],
        out_specs=out_specs,
        out_shape=out_shape,
        scratch_shapes=[pltpu.VMEM((FFN_TM, D_MODEL), f32)],
        compiler_params=_cparams(("arbitrary", "arbitrary"), 52),
        name="ffn_final" if final else "ffn",
    )(h2, x1, wgu, wgu, wd, mod, gn, modn)


def _dft_cos_sin(n):
    k = np.arange(n, dtype=np.int64)
    ang = 2.0 * np.pi * ((k[:, None] * k[None, :]) % n).astype(np.float64) / n
    return np.cos(ang), np.sin(ang)


def _fourier_consts(seq):
    cw, sw = _dft_cos_sin(FNET_GW)
    cs, ss = _dft_cos_sin(seq)
    scale = 1.0 / np.sqrt(float(seq * FNET_GW))
    chan = np.concatenate([cw, sw], axis=1).astype(np.float32)
    pos = (np.concatenate([cs, -ss], axis=1) * scale).astype(np.float32)
    return jnp.asarray(chan).astype(bf16), jnp.asarray(pos).astype(bf16)


def _path(x, h, batch, seq, mods, h0s, p):
    depth = len(mods)
    chan_dft, pos_dft = _fourier_consts(seq)
    states = []
    for l in range(depth):
        u = _inproj_call(h, p["w_in"], l)
        ma, hl = _mixa_call(u, h0s[l], p["conv_a_w"][l], p["conv_a_b"][l][None], p["wg"][l], p["bg"][l],
                            p["lru_lam"][l], p["out_norm"][l][None, :D_A], batch, seq)
        mb = _mixb_call(u, chan_dft, pos_dft, p["out_norm"][l][None, D_A:D_A + D_B], batch, seq)
        mc = _mixc_call(u, p["conv_c_w"][l], p["conv_c_b"][l][None], p["ln_c_g"][l][None], p["ln_c_b"][l][None],
                        p["out_norm"][l][None, D_A + D_B:], batch, seq)
        x1, h2 = _outproj_call(x, ma, mb, mc, p["w_out"], l, p["norm_ffn"][l][None], mods[l], seq)
        states.append(hl)
        if l + 1 < depth:
            x, h = _ffn_call(h2, x1, p["w_gu"], p["w_down"], l, mods[l], p["norm_mix"][l + 1][None],
                             mods[l + 1], seq, final=False)
        else:
            (y,) = _ffn_call(h2, x1, p["w_gu"], p["w_down"], l, mods[l], p["final_norm"][None],
                             mods[l], seq, final=True)
    return y, states


def kernel(x_prompt, x_sample, c, state_lru, c_ctx, w_mod, b_mod, norm_mix, norm_ffn, w_in, conv_a_w, conv_a_b,
           lru_wa, lru_ba, lru_wx, lru_bx, lru_lam, conv_c_w, conv_c_b, ln_c_g, ln_c_b, out_norm, w_out, w_gu,
           w_down, final_norm):
    depth = w_in.shape[0]
    bp, sp, _ = x_prompt.shape
    bs, ss, _ = x_sample.shape
    assert bs + 1 <= SUBLANES

    cv = jnp.concatenate([c_ctx[None], c, jnp.zeros((SUBLANES - 1 - bs, D_MODEL), f32)], axis=0)
    mod = _mod_call(cv, w_mod, b_mod).reshape(depth, SUBLANES, N_MOD, D_MODEL)
    mods_ctx = [mod[l, 0:1] for l in range(depth)]
    mods_lat = [mod[l, 1:1 + bs] for l in range(depth)]

    wg = (0.5 * jnp.concatenate([lru_wa[:, 0], lru_wa[:, 1], lru_wx[:, 0], lru_wx[:, 1]], axis=-1)).astype(bf16)
    blk = lambda v: v.reshape(depth, LRU_BLOCKS, 1, LRU_BW)
    bg = 0.5 * jnp.concatenate([blk(lru_ba[:, 0]), blk(lru_ba[:, 1]), blk(lru_bx[:, 0]), blk(lru_bx[:, 1])], axis=-1)
    p = dict(w_in=w_in.astype(bf16), w_out=w_out.astype(bf16), w_gu=w_gu.astype(bf16), w_down=w_down.astype(bf16),
             conv_a_w=conv_a_w, conv_a_b=conv_a_b, wg=wg, bg=bg, lru_lam=lru_lam, conv_c_w=conv_c_w,
             conv_c_b=conv_c_b, ln_c_g=ln_c_g, ln_c_b=ln_c_b, out_norm=out_norm, norm_ffn=norm_ffn,
             norm_mix=norm_mix, final_norm=final_norm)

    xp = x_prompt.reshape(bp * sp, D_MODEL)
    hp = _pre_ctx_call(xp, norm_mix[0][None], mods_ctx[0], sp)
    h0_ctx = jnp.zeros((bp, 2, D_A), f32)
    y_p, states = _path(xp, hp, bp, sp, mods_ctx, [h0_ctx] * depth, p)

    nf = D_MODEL // 4
    omega = (1.0 / (10000.0 ** (jnp.arange(nf, dtype=f32) / nf)))[None]
    xs, hs = _pre_lat_call(x_sample.reshape(bs * ss, D_MODEL), omega, norm_mix[0][None], mods_lat[0], ss)
    y_s, _ = _path(xs, hs, bs, ss, mods_lat, [state_lru[:, l] for l in range(depth)], p)

    return (y_p.reshape(bp, sp, D_MODEL), y_s.reshape(bs, ss, D_MODEL), jnp.stack(states, axis=1))
```

```python
import functools

import numpy as np
import jax
import jax.numpy as jnp
from jax import lax
from jax.experimental import pallas as pl
from jax.experimental.pallas import tpu as pltpu

f32 = jnp.float32
bf16 = jnp.bfloat16

D_MODEL = 2048
D_A = 1024
D_B = 512
D_C = 512
LRU_BLOCKS = 8
LRU_BW = 128
LRU_C = 8.0
CONV_A_W = 4
FNET_GROUPS = 4
FNET_GW = 128
CONV_C_W = 31
D_IN = 2 * D_A + D_B + 2 * D_C
D_FF = 5632
N_MOD = 6
GRID_W = 64
EPS = 1e-6

LANES = 128
SUBLANES = 8
MIB = 1024 * 1024

SHIFT1, SCALE1, GATE1, SHIFT2, SCALE2, GATE2 = range(N_MOD)


def _cparams(sem, vmem_mib):
    return pltpu.CompilerParams(dimension_semantics=sem, vmem_limit_bytes=vmem_mib * MIB)


def _sigmoid(x):
    return 0.5 * jnp.tanh(0.5 * x) + 0.5


def _rms_scale(x, g):
    ms = jnp.mean(x * x, axis=-1, keepdims=True)
    return x * lax.rsqrt(ms + EPS) * g


def _normmod(x, g, scale, shift):
    return _rms_scale(x, g) * (1.0 + scale) + shift


def _gelu_tanh(x):
    return x * (0.5 * (1.0 + jnp.tanh(np.sqrt(2.0 / np.pi).astype(np.float32) * (x + 0.044715 * (x * x * x)))))


def _shift_down(x):
    row = lax.broadcasted_iota(jnp.int32, x.shape, 0)
    return jnp.where(row == 0, 0.0, pltpu.roll(x, 1, 0))


def _shift_up(x):
    row = lax.broadcasted_iota(jnp.int32, x.shape, 0)
    return jnp.where(row == SUBLANES - 1, 0.0, pltpu.roll(x, SUBLANES - 1, 0))


MOD_TN = 1024


def _mod_body(cv_ref, w_ref, b_ref, o_ref):
    cv = cv_ref[...]
    s = (cv * _sigmoid(cv)).astype(bf16)
    o_ref[...] = jnp.dot(s, w_ref[...].astype(bf16), preferred_element_type=f32) + b_ref[...]


def _mod_call(cv, w_mod, b_mod):
    depth = w_mod.shape[0]
    n = N_MOD * D_MODEL
    return pl.pallas_call(
        _mod_body,
        grid=(depth, n // MOD_TN),
        in_specs=[
            pl.BlockSpec((SUBLANES, D_MODEL), lambda l, j: (0, 0)),
            pl.BlockSpec((None, D_MODEL, MOD_TN), lambda l, j: (l, 0, j)),
            pl.BlockSpec((None, 1, MOD_TN), lambda l, j: (l, 0, j)),
        ],
        out_specs=pl.BlockSpec((None, SUBLANES, MOD_TN), lambda l, j: (l, 0, j)),
        out_shape=jax.ShapeDtypeStruct((depth, SUBLANES, n), f32),
        compiler_params=_cparams(("arbitrary", "arbitrary"), 40),
        name="mod_vectors",
    )(cv, w_mod, b_mod.reshape(depth, 1, n))


PRE_TM = 512


def _mod_spec(tm, seq, nmod):
    if nmod == 1:
        return pl.BlockSpec((None, N_MOD, D_MODEL), lambda i, *_: (0, 0, 0))
    return pl.BlockSpec((None, N_MOD, D_MODEL), lambda i, *_: ((i * tm) // seq, 0, 0))


def _pre_ctx_body(x_ref, g_ref, mod_ref, h_ref):
    h_ref[...] = _normmod(x_ref[...], g_ref[...], mod_ref[SCALE1:SCALE1 + 1, :],
                          mod_ref[SHIFT1:SHIFT1 + 1, :]).astype(bf16)


def _pre_ctx_call(x, g, mod, seq):
    rows = x.shape[0]
    assert rows % PRE_TM == 0
    return pl.pallas_call(
        _pre_ctx_body,
        grid=(rows // PRE_TM,),
        in_specs=[
            pl.BlockSpec((PRE_TM, D_MODEL), lambda i: (i, 0)),
            pl.BlockSpec((1, D_MODEL), lambda i: (0, 0)),
            _mod_spec(PRE_TM, seq, mod.shape[0]),
        ],
        out_specs=pl.BlockSpec((PRE_TM, D_MODEL), lambda i: (i, 0)),
        out_shape=jax.ShapeDtypeStruct((rows, D_MODEL), bf16),
        compiler_params=_cparams(("arbitrary",), 32),
        name="pre_ctx",
    )(x, g, mod)


def _pre_lat_body(x_ref, om_ref, g_ref, mod_ref, x0_ref, h_ref, *, tiles_per_seq):
    nf = D_MODEL // 4
    grid_rows = PRE_TM // GRID_W
    i = pl.program_id(0)
    r_base = (i % tiles_per_seq) * grid_rows
    om = om_ref[...]
    ang_r = (r_base + lax.broadcasted_iota(jnp.int32, (grid_rows, nf), 0)).astype(f32) * om
    sin_r, cos_r = jnp.sin(ang_r), jnp.cos(ang_r)
    ang_c = lax.broadcasted_iota(jnp.int32, (GRID_W, nf), 0).astype(f32) * om
    sin_c, cos_c = jnp.sin(ang_c), jnp.cos(ang_c)
    g = g_ref[...]
    scale = mod_ref[SCALE1:SCALE1 + 1, :]
    shift = mod_ref[SHIFT1:SHIFT1 + 1, :]
    for q in range(grid_rows):
        rows = slice(q * GRID_W, (q + 1) * GRID_W)
        pos = jnp.concatenate([jnp.broadcast_to(sin_r[q:q + 1], (GRID_W, nf)),
                               jnp.broadcast_to(cos_r[q:q + 1], (GRID_W, nf)), sin_c, cos_c], axis=-1)
        x0 = x_ref[rows, :] + pos
        x0_ref[rows, :] = x0
        h_ref[rows, :] = _normmod(x0, g, scale, shift).astype(bf16)


def _pre_lat_call(x, omega, g, mod, seq):
    rows = x.shape[0]
    assert seq % PRE_TM == 0 and PRE_TM % GRID_W == 0 and rows % PRE_TM == 0
    row_spec = pl.BlockSpec((PRE_TM, D_MODEL), lambda i: (i, 0))
    return pl.pallas_call(
        functools.partial(_pre_lat_body, tiles_per_seq=seq // PRE_TM),
        grid=(rows // PRE_TM,),
        in_specs=[
            row_spec,
            pl.BlockSpec((1, D_MODEL // 4), lambda i: (0, 0)),
            pl.BlockSpec((1, D_MODEL), lambda i: (0, 0)),
            _mod_spec(PRE_TM, seq, mod.shape[0]),
        ],
        out_specs=[row_spec, row_spec],
        out_shape=[jax.ShapeDtypeStruct((rows, D_MODEL), f32), jax.ShapeDtypeStruct((rows, D_MODEL), bf16)],
        compiler_params=_cparams(("arbitrary",), 40),
        name="pre_lat",
    )(x, omega, g, mod)


INPROJ_TM = 1024
INPROJ_TN = 512


def _inproj_body(h_ref, w_ref, *rest, ncast):
    srcs, o_ref, dsts = rest[:ncast], rest[ncast], rest[ncast + 1:]
    o_ref[...] = jnp.dot(h_ref[...], w_ref[...], preferred_element_type=f32)
    for src, dst in zip(srcs, dsts):
        dst[...] = src[...].astype(bf16)


def _inproj_call(h, w, casts=()):
    rows = h.shape[0]
    assert rows % INPROJ_TM == 0
    ni, nj = rows // INPROJ_TM, D_IN // INPROJ_TN
    in_specs = [
        pl.BlockSpec((INPROJ_TM, D_MODEL), lambda i, j: (i, 0)),
        pl.BlockSpec((D_MODEL, INPROJ_TN), lambda i, j: (0, j)),
    ]
    out_specs = [pl.BlockSpec((INPROJ_TM, INPROJ_TN), lambda i, j: (i, j))]
    out_shape = [jax.ShapeDtypeStruct((rows, D_IN), f32)]
    for arr, layer, axis, n in casts:
        _, r, c = arr.shape
        assert n <= ni * nj and (r, c)[axis] % n == 0
        chunk = lambda i, j, n=n: jnp.minimum(i * nj + j, n - 1)
        if axis == 1:
            blk = (r, c // n)
            in_specs.append(pl.BlockSpec((None,) + blk, lambda i, j, layer=layer, chunk=chunk: (layer, 0, chunk(i, j))))
            out_specs.append(pl.BlockSpec(blk, lambda i, j, chunk=chunk: (0, chunk(i, j))))
        else:
            blk = (r // n, c)
            in_specs.append(pl.BlockSpec((None,) + blk, lambda i, j, layer=layer, chunk=chunk: (layer, chunk(i, j), 0)))
            out_specs.append(pl.BlockSpec(blk, lambda i, j, chunk=chunk: (chunk(i, j), 0)))
        out_shape.append(jax.ShapeDtypeStruct((r, c), bf16))
    return pl.pallas_call(
        functools.partial(_inproj_body, ncast=len(casts)),
        grid=(ni, nj),
        in_specs=in_specs,
        out_specs=out_specs,
        out_shape=out_shape,
        compiler_params=_cparams(("arbitrary", "arbitrary"), 44),
        name="in_proj_cast" if casts else "in_proj",
    )(h, w, *[arr for arr, *_ in casts])


SEG_CH = 32
A_GM = 256
A_GS = A_GM // SUBLANES
A_ES = 8
A_EC = A_ES * SUBLANES
A_OC = 128
SQRT_FLOOR = 1e-37


def _mixa_body(xa_ref, ya_ref, h0_ref, cw_ref, cb_ref, wg_ref, bg_ref, lam_ref, gn_ref,
               ma_ref, hl_ref, xe_ref, af_ref, uf_ref, ab_ref, ub_ref, *, seq):
    seg = seq // SUBLANES
    dirs = ((af_ref, uf_ref), (ab_ref, ub_ref))
    tile = lambda: jnp.zeros((SUBLANES, LANES), f32)

    for n in range(LRU_BLOCKS):
        ln = slice(n * LANES, (n + 1) * LANES)
        for i0 in range(0, seg, SEG_CH):
            v = jnp.stack([xa_ref[j * seg + i0:j * seg + i0 + SEG_CH, ln] for j in range(SUBLANES)], axis=0)
            xe_ref[n, 2 + i0:2 + i0 + SEG_CH] = jnp.swapaxes(v, 0, 1)
        xe_ref[n, 0] = _shift_down(xe_ref[n, seg])
        xe_ref[n, 1] = _shift_down(xe_ref[n, seg + 1])
        xe_ref[n, seg + 2] = _shift_up(xe_ref[n, 2])

    lam = lam_ref[...]
    half_rate = (-0.5 * LRU_C) * (jnp.maximum(-lam, 0.0) + jnp.log1p(jnp.exp(-jnp.abs(lam))))

    def gate_chunk(c, state):
        base = c * A_GS
        new_state = []
        for n in range(LRU_BLOCKS):
            ln = slice(n * LANES, (n + 1) * LANES)
            xac = jnp.broadcast_to(cb_ref[:, ln], (A_GS, SUBLANES, LANES))
            for k in range(CONV_A_W):
                xac = xac + cw_ref[k:k + 1, ln] * xe_ref[n, pl.ds(base + k, A_GS)]
            gates = jnp.dot(xac.reshape(A_GM, LANES).astype(bf16), wg_ref[n],
                            preferred_element_type=f32) + bg_ref[n]
            hf, pf, hb, pb = state[n]
            for s in range(A_GS // A_ES):
                rows = slice(s * A_EC, (s + 1) * A_EC)
                steps = slice(s * A_ES, (s + 1) * A_ES)
                xh = 0.5 * xac[steps]
                au = []
                for d, (a_ref, u_ref) in enumerate(dirs):
                    ga = gates[rows, d * LANES:(d + 1) * LANES]
                    gx = gates[rows, (2 + d) * LANES:(3 + d) * LANES]
                    rate = half_rate[d:d + 1, ln]
                    log_a = rate * jnp.tanh(ga) + rate
                    a = jnp.exp(log_a)
                    var = (-1.0 - a * a) * jnp.tanh(log_a)
                    std = var * lax.rsqrt(jnp.maximum(var, SQRT_FLOOR))
                    a = a.reshape(A_ES, SUBLANES, LANES)
                    u = (std * (1.0 + jnp.tanh(gx))).reshape(A_ES, SUBLANES, LANES) * xh
                    a_ref[n, pl.ds(base + s * A_ES, A_ES)] = a
                    u_ref[n, pl.ds(base + s * A_ES, A_ES)] = u
                    au.append((a, u))
                for i in range(A_ES):
                    a, u = au[0]
                    hf = a[i] * hf + u[i]
                    pf = a[i] * pf
                    a, u = au[1]
                    hb = hb + pb * u[i]
                    pb = pb * a[i]
            new_state.append((hf, pf, hb, pb))
        return tuple(new_state)

    init = tuple((tile(), tile() + 1.0, tile(), tile() + 1.0) for _ in range(LRU_BLOCKS))
    ends = lax.fori_loop(0, seq // A_GM, gate_chunk, init)

    carries_f, carries_b = [], []
    for n in range(LRU_BLOCKS):
        ln = slice(n * LANES, (n + 1) * LANES)
        hf, pf, hb, pb = ends[n]
        c = h0_ref[0:1, ln]
        rows = []
        for j in range(SUBLANES):
            rows.append(c)
            c = pf[j:j + 1] * c + hf[j:j + 1]
        hl_ref[0:1, ln] = c
        carries_f.append(jnp.concatenate(rows, axis=0))
        c = h0_ref[1:2, ln]
        rows = [None] * SUBLANES
        for j in range(SUBLANES - 1, -1, -1):
            rows[j] = c
            c = pb[j:j + 1] * c + hb[j:j + 1]
        hl_ref[1:2, ln] = c
        carries_b.append(jnp.concatenate(rows, axis=0))

    hf, hb = carries_f, carries_b
    for i in range(seg):
        ib = seg - 1 - i
        for n in range(LRU_BLOCKS):
            hf[n] = af_ref[n, i] * hf[n] + uf_ref[n, i]
            uf_ref[n, i] = hf[n]
            hb[n] = ab_ref[n, ib] * hb[n] + ub_ref[n, ib]
            ub_ref[n, ib] = hb[n]

    for n in range(LRU_BLOCKS):
        for i0 in range(0, seg, SEG_CH):
            w = jnp.swapaxes(uf_ref[n, i0:i0 + SEG_CH] + ub_ref[n, i0:i0 + SEG_CH], 0, 1)
            for j in range(SUBLANES):
                t0 = (j * seg + i0) // SUBLANES
                af_ref[n, t0:t0 + SEG_CH // SUBLANES] = w[j].reshape(SEG_CH // SUBLANES, SUBLANES, LANES)

    gn = gn_ref[...]

    def out_pass(c, carry):
        r0 = pl.multiple_of(c * A_OC, A_OC)
        t0 = c * (A_OC // SUBLANES)
        outs = []
        sq = jnp.zeros((A_OC, LANES), f32)
        for n in range(LRU_BLOCKS):
            ln = slice(n * LANES, (n + 1) * LANES)
            h = af_ref[n, pl.ds(t0, A_OC // SUBLANES)].reshape(A_OC, LANES)
            o = h * _gelu_tanh(ya_ref[pl.ds(r0, A_OC), ln])
            sq = sq + o * o
            outs.append(o)
        inv = lax.rsqrt(jnp.sum(sq, axis=-1, keepdims=True) * (1.0 / D_A) + EPS)
        for n in range(LRU_BLOCKS):
            ln = slice(n * LANES, (n + 1) * LANES)
            ma_ref[pl.ds(r0, A_OC), ln] = (outs[n] * inv * gn[:, ln]).astype(bf16)
        return carry

    lax.fori_loop(0, seq // A_OC, out_pass, 0)


def _mixa_call(u, h0, cw, cb, wg, bg, lam, gn, batch, seq):
    rows = batch * seq
    seg = seq // SUBLANES
    assert seq % A_GM == 0 and seg % SEG_CH == 0
    small = lambda shape: pl.BlockSpec(shape, lambda b: (0,) * len(shape))
    seg_buf = pltpu.VMEM((LRU_BLOCKS, seg, SUBLANES, LANES), f32)
    return pl.pallas_call(
        functools.partial(_mixa_body, seq=seq),
        grid=(batch,),
        in_specs=[
            pl.BlockSpec((seq, D_A), lambda b: (b, 0)),
            pl.BlockSpec((seq, D_A), lambda b: (b, 1)),
            pl.BlockSpec((None, 2, D_A), lambda b: (b, 0, 0)),
            small((CONV_A_W, D_A)),
            small((1, D_A)),
            small((LRU_BLOCKS, LRU_BW, 4 * LRU_BW)),
            small((LRU_BLOCKS, 1, 4 * LRU_BW)),
            small((2, D_A)),
            small((1, D_A)),
        ],
        out_specs=[
            pl.BlockSpec((seq, D_A), lambda b: (b, 0)),
            pl.BlockSpec((None, 2, D_A), lambda b: (b, 0, 0)),
        ],
        out_shape=[jax.ShapeDtypeStruct((rows, D_A), bf16), jax.ShapeDtypeStruct((batch, 2, D_A), f32)],
        scratch_shapes=[
            pltpu.VMEM((LRU_BLOCKS, seg + CONV_A_W - 1, SUBLANES, LANES), f32),
            seg_buf, seg_buf, seg_buf, seg_buf,
        ],
        compiler_params=_cparams(("arbitrary",), 52),
        name="mix_lru",
    )(u, u, h0, cw, cb, wg, bg, lam, gn)


B_RC = 256


B_ROWS = 1024


def _mixb_body(xb_ref, cw_ref, ds_ref, gn_ref, mb_ref, t_ref, *, seq):
    gn = gn_ref[...]
    for s in range(B_ROWS // seq):
        r0 = s * seq
        for g in range(FNET_GROUPS):
            ln = slice(g * FNET_GW, (g + 1) * FNET_GW)
            y = jnp.dot(xb_ref[r0:r0 + seq, ln].astype(bf16), cw_ref[...], preferred_element_type=f32)
            t_ref[s, 0:seq, ln] = y[:, :FNET_GW].astype(bf16)
            t_ref[s, seq:2 * seq, ln] = y[:, FNET_GW:].astype(bf16)
        for c in range(seq // B_RC):
            rows = slice(c * B_RC, (c + 1) * B_RC)
            o = jnp.dot(ds_ref[rows, :], t_ref[s], preferred_element_type=f32)
            mb_ref[r0 + c * B_RC:r0 + (c + 1) * B_RC, :] = _rms_scale(o, gn).astype(bf16)


def _mixb_call(u, cw, ds, gn, batch, seq):
    rows = batch * seq
    assert seq % B_RC == 0 and B_ROWS % seq == 0 and rows % B_ROWS == 0
    small = lambda shape: pl.BlockSpec(shape, lambda b: (0,) * len(shape))
    return pl.pallas_call(
        functools.partial(_mixb_body, seq=seq),
        grid=(rows // B_ROWS,),
        in_specs=[
            pl.BlockSpec((B_ROWS, D_B), lambda b: (b, 2 * D_A // D_B)),
            small((FNET_GW, 2 * FNET_GW)),
            small((seq, 2 * seq)),
            small((1, D_B)),
        ],
        out_specs=pl.BlockSpec((B_ROWS, D_B), lambda b: (b, 0)),
        out_shape=jax.ShapeDtypeStruct((rows, D_B), bf16),
        scratch_shapes=[pltpu.VMEM((B_ROWS // seq, 2 * seq, D_B), bf16)],
        compiler_params=_cparams(("arbitrary",), 40),
        name="mix_fourier",
    )(u, cw, ds, gn)


C_HALF = (CONV_C_W - 1) // 2
C_RS = 4
C_NS = 32
C_NBLK = D_C // LANES
C_PARTS = 4


def _mixc_body(xv_ref, xg_ref, w_ref, b_ref, lg_ref, lb_ref, gn_ref, mc_ref, ve_ref, wb_ref, op_ref, *, seq):
    seg = seq // SUBLANES
    for q in range(C_NBLK):
        ln = slice(q * LANES, (q + 1) * LANES)
        for i0 in range(0, seg, SEG_CH):
            parts = []
            for j in range(SUBLANES):
                rows = slice(j * seg + i0, j * seg + i0 + SEG_CH)
                parts.append(xv_ref[rows, ln] * _sigmoid(xg_ref[rows, ln]))
            ve_ref[q, C_HALF + i0:C_HALF + i0 + SEG_CH] = jnp.swapaxes(jnp.stack(parts, axis=0), 0, 1)
        for m in range(1, C_HALF + 1):
            ve_ref[q, C_HALF - m] = _shift_down(ve_ref[q, C_HALF + seg - m])
            ve_ref[q, C_HALF + seg - 1 + m] = _shift_up(ve_ref[q, C_HALF + m - 1])
        for k in range(CONV_C_W):
            wb_ref[k, q] = jnp.broadcast_to(w_ref[k:k + 1, ln], (SUBLANES, LANES))

    lg, lb, gn = lg_ref[...], lb_ref[...], gn_ref[...]

    def conv(c, carry):
        base = c * C_RS
        for q in range(C_NBLK):
            ln = slice(q * LANES, (q + 1) * LANES)
            win = ve_ref[q, pl.ds(base, C_RS + CONV_C_W - 1)]
            parts = [jnp.broadcast_to(b_ref[:, ln], (C_RS, SUBLANES, LANES))] + [None] * (C_PARTS - 1)
            for k in range(CONV_C_W):
                term = wb_ref[k, q] * win[k:k + C_RS]
                r = k % C_PARTS
                parts[r] = term if parts[r] is None else parts[r] + term
            op_ref[q, pl.ds(base, C_RS)] = sum(parts[1:], parts[0])
        return carry

    lax.fori_loop(0, seg // C_RS, conv, 0)

    def norms(c, carry):
        base = c * C_NS
        ys = [op_ref[q, pl.ds(base, C_NS)].reshape(C_NS * SUBLANES, LANES) for q in range(C_NBLK)]
        lane_mean = lambda vs: jnp.sum(sum(vs[1:], vs[0]), axis=-1, keepdims=True) * (1.0 / D_C)
        mu = lane_mean(ys)
        ds_ = [y - mu for y in ys]
        inv = lax.rsqrt(lane_mean([d * d for d in ds_]) + EPS)
        outs = []
        for q in range(C_NBLK):
            ln = slice(q * LANES, (q + 1) * LANES)
            v = ds_[q] * inv * lg[:, ln] + lb[:, ln]
            outs.append(v * _sigmoid(v))
        inv2 = lax.rsqrt(lane_mean([o * o for o in outs]) + EPS)
        for q in range(C_NBLK):
            ln = slice(q * LANES, (q + 1) * LANES)
            op_ref[q, pl.ds(base, C_NS)] = (outs[q] * inv2 * gn[:, ln]).reshape(C_NS, SUBLANES, LANES)
        return carry

    lax.fori_loop(0, seg // C_NS, norms, 0)

    for q in range(C_NBLK):
        ln = slice(q * LANES, (q + 1) * LANES)
        for i0 in range(0, seg, SEG_CH):
            w = jnp.swapaxes(op_ref[q, i0:i0 + SEG_CH], 0, 1)
            for j in range(SUBLANES):
                mc_ref[j * seg + i0:j * seg + i0 + SEG_CH, ln] = w[j].astype(bf16)


def _mixc_call(u, w, b, lg, lb, gn, batch, seq):
    rows = batch * seq
    seg = seq // SUBLANES
    assert seg % SEG_CH == 0 and seg >= C_HALF
    small = lambda shape: pl.BlockSpec(shape, lambda i: (0,) * len(shape))
    col0 = (2 * D_A + D_B) // D_C
    return pl.pallas_call(
        functools.partial(_mixc_body, seq=seq),
        grid=(batch,),
        in_specs=[
            pl.BlockSpec((seq, D_C), lambda i: (i, col0)),
            pl.BlockSpec((seq, D_C), lambda i: (i, col0 + 1)),
            small((CONV_C_W, D_C)),
            small((1, D_C)),
            small((1, D_C)),
            small((1, D_C)),
            small((1, D_C)),
        ],
        out_specs=pl.BlockSpec((seq, D_C), lambda i: (i, 0)),
        out_shape=jax.ShapeDtypeStruct((rows, D_C), bf16),
        scratch_shapes=[
            pltpu.VMEM((C_NBLK, seg + CONV_C_W - 1, SUBLANES, LANES), f32),
            pltpu.VMEM((CONV_C_W, C_NBLK, SUBLANES, LANES), f32),
            pltpu.VMEM((C_NBLK, seg, SUBLANES, LANES), f32),
        ],
        compiler_params=_cparams(("arbitrary",), 32),
        name="mix_conv",
    )(u, u, w, b, lg, lb, gn)


OUT_TM = 512


def _outproj_body(x_ref, ma_ref, mb_ref, mc_ref, wo_ref, g_ref, mod_ref, x1_ref, h2_ref):
    acc = jnp.dot(ma_ref[...], wo_ref[0:D_A, :], preferred_element_type=f32)
    acc = acc + jnp.dot(mb_ref[...], wo_ref[D_A:D_A + D_B, :], preferred_element_type=f32)
    acc = acc + jnp.dot(mc_ref[...], wo_ref[D_A + D_B:, :], preferred_element_type=f32)
    x1 = x_ref[...] + mod_ref[GATE1:GATE1 + 1, :] * acc
    x1_ref[...] = x1
    h2_ref[...] = _normmod(x1, g_ref[...], mod_ref[SCALE2:SCALE2 + 1, :], mod_ref[SHIFT2:SHIFT2 + 1, :]).astype(bf16)


def _outproj_call(x, ma, mb, mc, wo, g, mod, seq):
    rows = x.shape[0]
    assert rows % OUT_TM == 0
    row = lambda w: pl.BlockSpec((OUT_TM, w), lambda i: (i, 0))
    return pl.pallas_call(
        _outproj_body,
        grid=(rows // OUT_TM,),
        in_specs=[
            row(D_MODEL), row(D_A), row(D_B), row(D_C),
            pl.BlockSpec((D_MODEL, D_MODEL), lambda i: (0, 0)),
            pl.BlockSpec((1, D_MODEL), lambda i: (0, 0)),
            _mod_spec(OUT_TM, seq, mod.shape[0]),
        ],
        out_specs=[row(D_MODEL), row(D_MODEL)],
        out_shape=[jax.ShapeDtypeStruct((rows, D_MODEL), f32), jax.ShapeDtypeStruct((rows, D_MODEL), bf16)],
        compiler_params=_cparams(("arbitrary",), 52),
        name="out_proj",
    )(x, ma, mb, mc, wo, g, mod)


FFN_TM = 512
FFN_TF = 512


def _ffn_body(h2_ref, x1_ref, wg_ref, wu_ref, wd_ref, mod_ref, gn_ref, modn_ref, *rest, final):
    if final:
        y_ref, acc_ref = rest
    else:
        x2_ref, hn_ref, acc_ref = rest
    f = pl.program_id(1)

    @pl.when(f == 0)
    def _():
        acc_ref[...] = jnp.zeros_like(acc_ref)

    h = h2_ref[...]
    g = jnp.dot(h, wg_ref[...], preferred_element_type=f32)
    u = jnp.dot(h, wu_ref[...], preferred_element_type=f32)
    act = (g * _sigmoid(g) * u).astype(bf16)
    acc_ref[...] += jnp.dot(act, wd_ref[...], preferred_element_type=f32)

    @pl.when(f == pl.num_programs(1) - 1)
    def _():
        x2 = x1_ref[...] + mod_ref[GATE2:GATE2 + 1, :] * acc_ref[...]
        if final:
            y_ref[...] = _rms_scale(x2, gn_ref[...])
        else:
            x2_ref[...] = x2
            hn_ref[...] = _normmod(x2, gn_ref[...], modn_ref[SCALE1:SCALE1 + 1, :],
                                   modn_ref[SHIFT1:SHIFT1 + 1, :]).astype(bf16)


def _ffn_call(h2, x1, wgu, wd, mod, gn, modn, seq, final):
    rows = h2.shape[0]
    assert rows % FFN_TM == 0
    nf = D_FF // FFN_TF
    row = pl.BlockSpec((FFN_TM, D_MODEL), lambda i, f: (i, 0))
    if final:
        out_specs = [row]
        out_shape = [jax.ShapeDtypeStruct((rows, D_MODEL), f32)]
    else:
        out_specs = [row, row]
        out_shape = [jax.ShapeDtypeStruct((rows, D_MODEL), f32), jax.ShapeDtypeStruct((rows, D_MODEL), bf16)]
    return pl.pallas_call(
        functools.partial(_ffn_body, final=final),
        grid=(rows // FFN_TM, nf),
        in_specs=[
            row, row,
            pl.BlockSpec((D_MODEL, FFN_TF), lambda i, f: (0, f)),
            pl.BlockSpec((D_MODEL, FFN_TF), lambda i, f: (0, nf + f)),
            pl.BlockSpec((FFN_TF, D_MODEL), lambda i, f: (f, 0)),
            _mod_spec(FFN_TM, seq, mod.shape[0]),
            pl.BlockSpec((1, D_MODEL), lambda i, f: (0, 0)),
            _mod_spec(FFN_TM, seq, modn.shape[0]),
        ],
        out_specs=out_specs,
        out_shape=out_shape,
        scratch_shapes=[pltpu.VMEM((FFN_TM, D_MODEL), f32)],
        compiler_params=_cparams(("arbitrary", "arbitrary"), 52),
        name="ffn_final" if final else "ffn",
    )(h2, x1, wgu, wgu, wd, mod, gn, modn)


def _dft_cos_sin(n):
    k = np.arange(n, dtype=np.int64)
    ang = 2.0 * np.pi * ((k[:, None] * k[None, :]) % n).astype(np.float64) / n
    return np.cos(ang), np.sin(ang)


def _fourier_consts(seq):
    cw, sw = _dft_cos_sin(FNET_GW)
    cs, ss = _dft_cos_sin(seq)
    scale = 1.0 / np.sqrt(float(seq * FNET_GW))
    chan = np.concatenate([cw, sw], axis=1).astype(np.float32)
    pos = (np.concatenate([cs, -ss], axis=1) * scale).astype(np.float32)
    return jnp.asarray(chan).astype(bf16), jnp.asarray(pos).astype(bf16)


CAST_PLAN = (("w_gu", 1, 44), ("w_down", 0, 44), ("w_out", 0, 32))
CAST_NEXT_IN = ("w_in", 0, 32)


def _path(x, h, batch, seq, mods, h0s, p, wts, f32_weights=None):
    depth = len(mods)
    chan_dft, pos_dft = _fourier_consts(seq)
    states = []
    for l in range(depth):
        if f32_weights is None:
            (u,) = _inproj_call(h, wts[l]["w_in"])
        else:
            plan = [(name, l, axis, n) for name, axis, n in CAST_PLAN]
            if l + 1 < depth:
                plan.append((CAST_NEXT_IN[0], l + 1) + CAST_NEXT_IN[1:])
            u, *made = _inproj_call(h, wts[l]["w_in"], [(f32_weights[name], layer, axis, n)
                                                          for name, layer, axis, n in plan])
            for (name, layer, _, _), arr in zip(plan, made):
                wts[layer][name] = arr
        ma, hl = _mixa_call(u, h0s[l], p["conv_a_w"][l], p["conv_a_b"][l][None], p["wg"][l], p["bg"][l],
                            p["lru_lam"][l], p["out_norm"][l][None, :D_A], batch, seq)
        mb = _mixb_call(u, chan_dft, pos_dft, p["out_norm"][l][None, D_A:D_A + D_B], batch, seq)
        mc = _mixc_call(u, p["conv_c_w"][l], p["conv_c_b"][l][None], p["ln_c_g"][l][None], p["ln_c_b"][l][None],
                        p["out_norm"][l][None, D_A + D_B:], batch, seq)
        x1, h2 = _outproj_call(x, ma, mb, mc, wts[l]["w_out"], p["norm_ffn"][l][None], mods[l], seq)
        states.append(hl)
        if l + 1 < depth:
            x, h = _ffn_call(h2, x1, wts[l]["w_gu"], wts[l]["w_down"], mods[l], p["norm_mix"][l + 1][None],
                             mods[l + 1], seq, final=False)
        else:
            (y,) = _ffn_call(h2, x1, wts[l]["w_gu"], wts[l]["w_down"], mods[l], p["final_norm"][None],
                             mods[l], seq, final=True)
    return y, states


def kernel(x_prompt, x_sample, c, state_lru, c_ctx, w_mod, b_mod, norm_mix, norm_ffn, w_in, conv_a_w, conv_a_b,
           lru_wa, lru_ba, lru_wx, lru_bx, lru_lam, conv_c_w, conv_c_b, ln_c_g, ln_c_b, out_norm, w_out, w_gu,
           w_down, final_norm):
    depth = w_in.shape[0]
    bp, sp, _ = x_prompt.shape
    bs, ss, _ = x_sample.shape
    assert bs + 1 <= SUBLANES

    cv = jnp.concatenate([c_ctx[None], c, jnp.zeros((SUBLANES - 1 - bs, D_MODEL), f32)], axis=0)
    mod = _mod_call(cv, w_mod, b_mod).reshape(depth, SUBLANES, N_MOD, D_MODEL)
    mods_ctx = [mod[l, 0:1] for l in range(depth)]
    mods_lat = [mod[l, 1:1 + bs] for l in range(depth)]

    wg = (0.5 * jnp.concatenate([lru_wa[:, 0], lru_wa[:, 1], lru_wx[:, 0], lru_wx[:, 1]], axis=-1)).astype(bf16)
    blk = lambda v: v.reshape(depth, LRU_BLOCKS, 1, LRU_BW)
    bg = 0.5 * jnp.concatenate([blk(lru_ba[:, 0]), blk(lru_ba[:, 1]), blk(lru_bx[:, 0]), blk(lru_bx[:, 1])], axis=-1)
    p = dict(conv_a_w=conv_a_w, conv_a_b=conv_a_b, wg=wg, bg=bg, lru_lam=lru_lam, conv_c_w=conv_c_w,
             conv_c_b=conv_c_b, ln_c_g=ln_c_g, ln_c_b=ln_c_b, out_norm=out_norm, norm_ffn=norm_ffn,
             norm_mix=norm_mix, final_norm=final_norm)

    xp = x_prompt.reshape(bp * sp, D_MODEL)
    hp = _pre_ctx_call(xp, norm_mix[0][None], mods_ctx[0], sp)
    h0_ctx = jnp.zeros((bp, 2, D_A), f32)
    wts = [dict() for _ in range(depth)]
    wts[0]["w_in"] = w_in[0].astype(bf16)
    y_p, states = _path(xp, hp, bp, sp, mods_ctx, [h0_ctx] * depth, p, wts,
                        f32_weights=dict(w_in=w_in, w_out=w_out, w_gu=w_gu, w_down=w_down))

    nf = D_MODEL // 4
    omega = (1.0 / (10000.0 ** (jnp.arange(nf, dtype=f32) / nf)))[None]
    xs, hs = _pre_lat_call(x_sample.reshape(bs * ss, D_MODEL), omega, norm_mix[0][None], mods_lat[0], ss)
    y_s, _ = _path(xs, hs, bs, ss, mods_lat, [state_lru[:, l] for l in range(depth)], p, wts)

    return (y_p.reshape(bp, sp, D_MODEL), y_s.reshape(bs, ss, D_MODEL), jnp.stack(states, axis=1))
```
